```python
import jax, jax.numpy as jnp
from jax import lax
import numpy as np

D_MODEL = 2048
BATCH = 2
SEQ = 8192
DEPTH = 4

CHUNK = 64
HEAD_DIM = 128
D_MIX = D_MODEL
W_A = 6 * HEAD_DIM
W_B = 6 * HEAD_DIM
N_POOL_GROUPS = 4
POOL_WINDOWS = (2, 4, 8, 16)
G_C = HEAD_DIM
W_C = N_POOL_GROUPS * G_C
D_IN_TOT = 3 * W_A + 2 * W_B + W_C
K_SHORT = 3
K_CONFORMER = 31
K_FFN = 3
D_FF = 5632
RMS_EPS = 1e-6
LN_EPS = 1e-5

kernel_name = "hybrid_conv_pool_streaming_trunk"


def rms_norm(x, g):
    xf = x.astype(jnp.float32)
    y = xf * lax.rsqrt(jnp.mean(xf * xf, axis=-1, keepdims=True) + RMS_EPS)
    return (y * g.astype(jnp.float32)).astype(x.dtype)


def layer_norm(x, g, b):
    xf = x.astype(jnp.float32)
    mu = jnp.mean(xf, axis=-1, keepdims=True)
    var = jnp.mean(jnp.square(xf - mu), axis=-1, keepdims=True)
    y = (xf - mu) * lax.rsqrt(var + LN_EPS)
    return (y * g.astype(jnp.float32) + b.astype(jnp.float32)).astype(x.dtype)


def causal_depthwise_conv(x, w):
    k, c = w.shape
    return lax.conv_general_dilated(
        x, w[:, None, :].astype(x.dtype),
        window_strides=(1,), padding=[(k - 1, 0)],
        dimension_numbers=("NWC", "WIO", "NWC"),
        feature_group_count=c)


def trailing_mean_minus_self(x, window):
    s = x.shape[1]
    xf = x.astype(jnp.float32)
    c = jnp.cumsum(xf, axis=1)
    c_shift = jnp.pad(c, ((0, 0), (window, 0), (0, 0)))[:, :s]
    counts = jnp.minimum(jnp.arange(1, s + 1, dtype=jnp.float32), float(window))[None, :, None]
    return ((c - c_shift) / counts - xf).astype(x.dtype)


def short_gated_conv_mixer(u_a, conv_w):
    b_gate, c_gate, v = jnp.split(u_a, 3, axis=-1)
    return b_gate * causal_depthwise_conv(c_gate * v, conv_w)


def conformer_conv_mixer(u_b, conv_w, conv_b, ln_g, ln_b):
    val, gate = jnp.split(u_b, 2, axis=-1)
    glu = val * jax.nn.sigmoid(gate)
    y = causal_depthwise_conv(glu, conv_w) + conv_b
    y = layer_norm(y, ln_g, ln_b)
    return jax.nn.silu(y)


def multiscale_pool_mixer(u_c, pool_w, pool_scale):
    bsz, s, _ = u_c.shape
    groups = u_c.reshape(bsz, s, N_POOL_GROUPS, G_C)
    pooled = jnp.stack(
        [trailing_mean_minus_self(groups[:, :, gi], w) for gi, w in enumerate(POOL_WINDOWS)],
        axis=2)
    mixed = jnp.einsum("bsgc,gcd->bsgd", pooled, pool_w)
    return mixed.reshape(bsz, s, W_C) * pool_scale


def conv_gated_mlp(h, w_up, conv_w, conv_b, w_down):
    up = h @ w_up
    up = causal_depthwise_conv(up, conv_w) + conv_b
    gate, val = jnp.split(up, 2, axis=-1)
    return (jax.nn.silu(gate) * val) @ w_down


def setup_inputs(seed: int = 0) -> dict:
    key = jax.random.key(seed)
    ks = jax.random.split(key, 20)
    f32 = jnp.float32
    nrm = lambda k, shape, scale: jax.random.normal(k, shape, f32) * scale
    gain = lambda k, shape: 1.0 + 0.05 * jax.random.normal(k, shape, f32)
    return {
        "x": jax.random.normal(ks[0], (BATCH, SEQ, D_MODEL), f32),
        "norm_mix_pre": gain(ks[1], (DEPTH, D_MODEL)),
        "norm_mix_post": gain(ks[2], (DEPTH, D_MODEL)),
        "norm_ffn_pre": gain(ks[3], (DEPTH, D_MODEL)),
        "norm_ffn_post": gain(ks[4], (DEPTH, D_MODEL)),
        "w_in": nrm(ks[5], (DEPTH, D_MODEL, D_IN_TOT), D_MODEL ** -0.5),
        "conv_a_w": nrm(ks[6], (DEPTH, K_SHORT, W_A), K_SHORT ** -0.5),
        "conv_b_w": nrm(ks[7], (DEPTH, K_CONFORMER, W_B), K_CONFORMER ** -0.5),
        "conv_b_bias": nrm(ks[8], (DEPTH, W_B), 0.01),
        "ln_b_gain": gain(ks[9], (DEPTH, W_B)),
        "ln_b_bias": nrm(ks[10], (DEPTH, W_B), 0.01),
        "pool_w": nrm(ks[11], (DEPTH, N_POOL_GROUPS, G_C, G_C), G_C ** -0.5),
        "pool_scale": 1.0 + 0.1 * jax.random.normal(ks[12], (DEPTH, W_C), f32),
        "w_out": nrm(ks[13], (DEPTH, D_MIX, D_MODEL), D_MIX ** -0.5),
        "w_up": nrm(ks[14], (DEPTH, D_MODEL, 2 * D_FF), D_MODEL ** -0.5),
        "conv_ffn_w": nrm(ks[15], (DEPTH, K_FFN, 2 * D_FF), K_FFN ** -0.5),
        "conv_ffn_bias": nrm(ks[16], (DEPTH, 2 * D_FF), 0.01),
        "w_down": nrm(ks[17], (DEPTH, D_FF, D_MODEL), D_FF ** -0.5),
    }


def reference(x, norm_mix_pre, norm_mix_post, norm_ffn_pre, norm_ffn_post,
              w_in, conv_a_w, conv_b_w, conv_b_bias, ln_b_gain, ln_b_bias,
              pool_w, pool_scale, w_out, w_up, conv_ffn_w, conv_ffn_bias, w_down):
    for l in range(DEPTH):
        h = rms_norm(x, norm_mix_pre[l])
        u = h @ w_in[l]
        u_a = u[..., :3 * W_A]
        u_b = u[..., 3 * W_A:3 * W_A + 2 * W_B]
        u_c = u[..., 3 * W_A + 2 * W_B:]
        y_a = short_gated_conv_mixer(u_a, conv_a_w[l])
        y_b = conformer_conv_mixer(u_b, conv_b_w[l], conv_b_bias[l], ln_b_gain[l], ln_b_bias[l])
        y_c = multiscale_pool_mixer(u_c, pool_w[l], pool_scale[l])
        y = jnp.concatenate([y_a, y_b, y_c], axis=-1) @ w_out[l]
        x = x + rms_norm(y, norm_mix_post[l])
        h = rms_norm(x, norm_ffn_pre[l])
        f = conv_gated_mlp(h, w_up[l], conv_ffn_w[l], conv_ffn_bias[l], w_down[l])
        x = x + rms_norm(f, norm_ffn_post[l])
    return x
```

```python
import functools

import jax
import jax.numpy as jnp
from jax import lax
from jax.experimental import pallas as pl
from jax.experimental.pallas import tpu as pltpu

HEAD_DIM = 128
N_HEADS_A = 6
N_HEADS_B = 6
POOL_WINDOWS = (2, 4, 8, 16)
RMS_EPS = 1e-6
LN_EPS = 1e-5

SUBLANES = 8
V7X_VMEM_BYTES = 64 * 1024 * 1024
VMEM_LIMIT_BYTES = V7X_VMEM_BYTES - 6 * 1024 * 1024

GROUP = 256
FFN_GROUPS = 2
FFN_CHUNK = 512


def _rms_norm(x, g):
    ms = jnp.mean(x * x, axis=-1, keepdims=True)
    return x * lax.rsqrt(ms + RMS_EPS) * g


def _sigmoid(x):
    return 1.0 / (1.0 + jnp.exp(-x))


def _store_group(e_ref, q, cur, prev_tail):
    group, width = cur.shape
    halo = e_ref.shape[1] - group
    shape3 = (halo // SUBLANES, SUBLANES, width)
    sub = lax.broadcasted_iota(jnp.int32, shape3, 1)
    mixed = jnp.where(sub == SUBLANES - 1, prev_tail.reshape(shape3),
                      cur[group - halo:, :].reshape(shape3))
    e_ref[q, 0:halo, :] = pltpu.roll(mixed, 1, axis=1).reshape(halo, width)
    e_ref[q, halo:halo + group, :] = cur


def _delayed(e_ref, delay, lanes=slice(None)):
    n_groups, rows, _ = e_ref.shape
    start = rows - GROUP - SUBLANES * delay
    out = e_ref[:, start:start + GROUP, lanes]
    return out.reshape(n_groups * GROUP, out.shape[-1])


def _causal_taps(e_ref, w_ref):
    n_taps = w_ref.shape[0]
    acc = None
    for k in range(n_taps):
        term = w_ref[k:k + 1, :] * _delayed(e_ref, n_taps - 1 - k)
        acc = term if acc is None else acc + term
    return acc


def _tail_rows(e_ref):
    halo = e_ref.shape[1] - GROUP
    return slice(GROUP, GROUP + halo)


def _clear_history(e_ref):
    tail = _tail_rows(e_ref)
    e_ref[e_ref.shape[0] - 1, tail, :] = jnp.zeros((tail.stop - tail.start, e_ref.shape[2]), jnp.float32)


def _history(e_ref):
    return e_ref[e_ref.shape[0] - 1, _tail_rows(e_ref), :]


def _mixer_kernel(x_ref, gpre_ref, gpost_ref, win_ref, caw_ref, cbw_ref, cbb_ref, lng_ref, lnb_ref,
                  pw_ref, ps_ref, wout_ref, o_ref, a_e, b_e, c_e, ycat, *, groups_per_seq):
    wa = N_HEADS_A * HEAD_DIM
    wb = N_HEADS_B * HEAD_DIM
    wc = len(POOL_WINDOWS) * HEAD_DIM
    seq_group = pl.program_id(0) % groups_per_seq

    @pl.when(seq_group == 0)
    def _():
        for e_ref in (a_e, b_e, c_e):
            _clear_history(e_ref)

    x = x_ref[...]
    hb = _rms_norm(x, gpre_ref[...]).astype(jnp.bfloat16)

    u_a = jnp.dot(hb, win_ref[:, 0:3 * wa], preferred_element_type=jnp.float32)
    _store_group(a_e, 0, u_a[:, wa:2 * wa] * u_a[:, 2 * wa:3 * wa], _history(a_e))
    ycat[:, 0:wa] = (u_a[:, 0:wa] * _causal_taps(a_e, caw_ref)).astype(jnp.bfloat16)

    off_b = 3 * wa
    u_b = jnp.dot(hb, win_ref[:, off_b:off_b + 2 * wb], preferred_element_type=jnp.float32)
    _store_group(b_e, 0, u_b[:, 0:wb] * _sigmoid(u_b[:, wb:2 * wb]), _history(b_e))
    conv_b = _causal_taps(b_e, cbw_ref) + cbb_ref[...]
    mu = jnp.mean(conv_b, axis=-1, keepdims=True)
    cen = conv_b - mu
    var = jnp.mean(cen * cen, axis=-1, keepdims=True)
    ln = cen * lax.rsqrt(var + LN_EPS) * lng_ref[...] + lnb_ref[...]
    ycat[:, wa:wa + wb] = (ln * _sigmoid(ln)).astype(jnp.bfloat16)

    off_c = off_b + 2 * wb
    u_c = jnp.dot(hb, win_ref[:, off_c:off_c + wc], preferred_element_type=jnp.float32)
    _store_group(c_e, 0, u_c, _history(c_e))
    row = lax.broadcasted_iota(jnp.int32, (GROUP, 1), 0)
    pos = seq_group * GROUP + (row % SUBLANES) * (GROUP // SUBLANES) + row // SUBLANES
    for g, window in enumerate(POOL_WINDOWS):
        lanes = slice(g * HEAD_DIM, (g + 1) * HEAD_DIM)
        self_rows = _delayed(c_e, 0, lanes)
        total = self_rows
        for back in range(1, window):
            total = total + _delayed(c_e, back, lanes)
        counts = jnp.minimum(pos + 1, window).astype(jnp.float32)
        pooled = total / counts - self_rows
        mixed = jnp.dot(pooled.astype(jnp.bfloat16), pw_ref[g], preferred_element_type=jnp.float32)
        ycat[:, wa + wb + g * HEAD_DIM:wa + wb + (g + 1) * HEAD_DIM] = (
            mixed * ps_ref[:, lanes]).astype(jnp.bfloat16)

    y = jnp.dot(ycat[...], wout_ref[...], preferred_element_type=jnp.float32)
    o_ref[...] = x + _rms_norm(y, gpost_ref[...])


def _mixer_layer(xs, layer, p, seq_len):
    n_tok, d_model = xs.shape
    assert seq_len % GROUP == 0 and n_tok % seq_len == 0
    wa = N_HEADS_A * HEAD_DIM
    wb = N_HEADS_B * HEAD_DIM
    wc = len(POOL_WINDOWS) * HEAD_DIM
    d_in = p["w_in"].shape[-1]
    assert d_in == 3 * wa + 2 * wb + wc and p["w_out"].shape[1] == wa + wb + wc
    k_a = p["conv_a_w"].shape[1]
    k_b = p["conv_b_w"].shape[1]
    halo_a = SUBLANES * (k_a - 1)
    halo_b = SUBLANES * (k_b - 1)
    halo_c = SUBLANES * (max(POOL_WINDOWS) - 1)
    assert max(halo_a, halo_b, halo_c) <= GROUP

    def layer_block(shape, **kw):
        zeros = (0,) * len(shape)
        return pl.BlockSpec((None,) + shape, lambda i: (layer,) + zeros, **kw)

    resident = dict(pipeline_mode=pl.Buffered(1))
    tok_spec = pl.BlockSpec((GROUP, d_model), lambda i: (i, 0))
    return pl.pallas_call(
        functools.partial(_mixer_kernel, groups_per_seq=seq_len // GROUP),
        grid=(n_tok // GROUP,),
        in_specs=[
            tok_spec,
            layer_block((1, d_model)),
            layer_block((1, d_model)),
            layer_block((d_model, d_in), **resident),
            layer_block((k_a, wa)),
            layer_block((k_b, wb)),
            layer_block((1, wb)),
            layer_block((1, wb)),
            layer_block((1, wb)),
            layer_block((len(POOL_WINDOWS), HEAD_DIM, HEAD_DIM)),
            layer_block((1, wc)),
            layer_block((wa + wb + wc, d_model), **resident),
        ],
        out_specs=tok_spec,
        out_shape=jax.ShapeDtypeStruct((n_tok, d_model), jnp.float32),
        scratch_shapes=[
            pltpu.VMEM((1, halo_a + GROUP, wa), jnp.float32),
            pltpu.VMEM((1, halo_b + GROUP, wb), jnp.float32),
            pltpu.VMEM((1, halo_c + GROUP, wc), jnp.float32),
            pltpu.VMEM((GROUP, wa + wb + wc), jnp.bfloat16),
        ],
        compiler_params=pltpu.CompilerParams(
            dimension_semantics=("arbitrary",), vmem_limit_bytes=VMEM_LIMIT_BYTES),
        name="mixer",
    )(xs, p["norm_mix_pre"], p["norm_mix_post"], p["w_in"], p["conv_a_w"], p["conv_b_w"],
      p["conv_b_bias"], p["ln_b_gain"], p["ln_b_bias"], p["pool_w"], p["pool_scale"], p["w_out"])


def _ffn_kernel(x_ref, gpre_ref, gpost_ref, wg_ref, wv_ref, cwg_ref, cwv_ref, cbg_ref, cbv_ref,
                wd_ref, o_ref, h_scr, g_e, v_e, carry, *, tiles_per_seq):
    n_groups = g_e.shape[0]
    halo = g_e.shape[1] - GROUP
    i = pl.program_id(0)
    j = pl.program_id(1)

    @pl.when(j == 0)
    def _():
        h_scr[...] = _rms_norm(x_ref[...], gpre_ref[...]).astype(jnp.bfloat16)
        o_ref[...] = jnp.zeros(o_ref.shape, jnp.float32)

    @pl.when(i % tiles_per_seq == 0)
    def _():
        carry[j] = jnp.zeros(carry.shape[1:], jnp.float32)

    hb = h_scr[...]
    for e_ref, w_ref, slot in ((g_e, wg_ref, 0), (v_e, wv_ref, 1)):
        up = jnp.dot(hb, w_ref[...], preferred_element_type=jnp.float32)
        prev_tail = carry[j, slot]
        for q in range(n_groups):
            cur = up[q * GROUP:(q + 1) * GROUP, :]
            _store_group(e_ref, q, cur, prev_tail)
            prev_tail = cur[GROUP - halo:, :]
        carry[j, slot] = prev_tail

    gate = _causal_taps(g_e, cwg_ref) + cbg_ref[...]
    val = _causal_taps(v_e, cwv_ref) + cbv_ref[...]
    act = (gate * _sigmoid(gate) * val).astype(jnp.bfloat16)
    o_ref[...] += jnp.dot(act, wd_ref[...], preferred_element_type=jnp.float32)

    @pl.when(j == pl.num_programs(1) - 1)
    def _():
        o_ref[...] = x_ref[...] + _rms_norm(o_ref[...], gpost_ref[...])


def _ffn_layer(xs, layer, p, seq_len, n_groups, chunk):
    n_tok, d_model = xs.shape
    tile = n_groups * GROUP
    d_ff = p["w_down"].shape[1]
    assert seq_len % tile == 0 and n_tok % seq_len == 0 and d_ff % chunk == 0
    assert p["w_up"].shape[-1] == 2 * d_ff
    n_chunks = d_ff // chunk
    k_f = p["conv_ffn_w"].shape[1]
    halo = SUBLANES * (k_f - 1)
    assert halo <= GROUP

    tok_spec = pl.BlockSpec((tile, d_model), lambda i, j: (i, 0))
    vec_spec = pl.BlockSpec((None, 1, d_model), lambda i, j: (layer, 0, 0))

    def up_cols(rows, half):
        return pl.BlockSpec((None, rows, chunk), lambda i, j: (layer, 0, half * n_chunks + j))

    return pl.pallas_call(
        functools.partial(_ffn_kernel, tiles_per_seq=seq_len // tile),
        grid=(n_tok // tile, n_chunks),
        in_specs=[
            tok_spec, vec_spec, vec_spec,
            up_cols(d_model, 0), up_cols(d_model, 1),
            up_cols(k_f, 0), up_cols(k_f, 1),
            up_cols(1, 0), up_cols(1, 1),
            pl.BlockSpec((None, chunk, d_model), lambda i, j: (layer, j, 0)),
        ],
        out_specs=tok_spec,
        out_shape=jax.ShapeDtypeStruct((n_tok, d_model), jnp.float32),
        scratch_shapes=[
            pltpu.VMEM((tile, d_model), jnp.bfloat16),
            pltpu.VMEM((n_groups, halo + GROUP, chunk), jnp.float32),
            pltpu.VMEM((n_groups, halo + GROUP, chunk), jnp.float32),
            pltpu.VMEM((n_chunks, 2, halo, chunk), jnp.float32),
        ],
        compiler_params=pltpu.CompilerParams(
            dimension_semantics=("arbitrary", "arbitrary"), vmem_limit_bytes=VMEM_LIMIT_BYTES),
        name="ffn",
    )(xs, p["norm_ffn_pre"], p["norm_ffn_post"], p["w_up"], p["w_up"], p["conv_ffn_w"],
      p["conv_ffn_w"], p["conv_ffn_bias"], p["conv_ffn_bias"], p["w_down"])


def _prepare_params(params):
    p = dict(params)
    for name in ("w_in", "w_out", "w_up", "w_down", "pool_w"):
        p[name] = params[name].astype(jnp.bfloat16)
    for name in ("norm_mix_pre", "norm_mix_post", "norm_ffn_pre", "norm_ffn_post", "conv_b_bias",
                 "ln_b_gain", "ln_b_bias", "pool_scale", "conv_ffn_bias"):
        p[name] = params[name][:, None, :]
    return p


def _to_stream_order(x2d):
    n_tok, d_model = x2d.shape
    per_sub = GROUP // SUBLANES
    return x2d.reshape(n_tok // GROUP, SUBLANES, per_sub, d_model).swapaxes(1, 2).reshape(n_tok, d_model)


def _from_stream_order(xs):
    n_tok, d_model = xs.shape
    per_sub = GROUP // SUBLANES
    return xs.reshape(n_tok // GROUP, per_sub, SUBLANES, d_model).swapaxes(1, 2).reshape(n_tok, d_model)


def _trunk(x, params, ffn_groups, ffn_chunk):
    batch, seq_len, d_model = x.shape
    p = _prepare_params(params)
    xs = _to_stream_order(x.reshape(batch * seq_len, d_model))
    for layer in range(params["w_in"].shape[0]):
        xs = _mixer_layer(xs, layer, p, seq_len)
        xs = _ffn_layer(xs, layer, p, seq_len, ffn_groups, ffn_chunk)
    return _from_stream_order(xs).reshape(batch, seq_len, d_model)


def kernel(x, norm_mix_pre, norm_mix_post, norm_ffn_pre, norm_ffn_post, w_in, conv_a_w, conv_b_w,
           conv_b_bias, ln_b_gain, ln_b_bias, pool_w, pool_scale, w_out, w_up, conv_ffn_w,
           conv_ffn_bias, w_down):
    params = dict(
        norm_mix_pre=norm_mix_pre, norm_mix_post=norm_mix_post, norm_ffn_pre=norm_ffn_pre,
        norm_ffn_post=norm_ffn_post, w_in=w_in, conv_a_w=conv_a_w, conv_b_w=conv_b_w,
        conv_b_bias=conv_b_bias, ln_b_gain=ln_b_gain, ln_b_bias=ln_b_bias, pool_w=pool_w,
        pool_scale=pool_scale, w_out=w_out, w_up=w_up, conv_ffn_w=conv_ffn_w,
        conv_ffn_bias=conv_ffn_bias, w_down=w_down)
    return _trunk(x, params, FFN_GROUPS, FFN_CHUNK)
```

```python
import functools

import jax
import jax.numpy as jnp
from jax import lax
from jax.experimental import pallas as pl
from jax.experimental.pallas import tpu as pltpu

HEAD_DIM = 128
N_HEADS_A = 6
N_HEADS_B = 6
POOL_WINDOWS = (2, 4, 8, 16)
RMS_EPS = 1e-6
LN_EPS = 1e-5

SUBLANES = 8
LANES = 128
V7X_VMEM_BYTES = 64 * 1024 * 1024
VMEM_LIMIT_BYTES = V7X_VMEM_BYTES - 6 * 1024 * 1024

GROUP = 256
FFN_GROUPS = 4
FFN_CHUNK = 512
NORM_ROWS = 16
EW_ROWS = 32
DOT_COLS = 512
OUT_COLS = 1024
A_PIECES = 3
ALL = slice(None)


def _blocks(n, size):
    return [slice(s, min(s + size, n)) for s in range(0, n, size)]


def _rms_norm(x, g):
    ms = jnp.mean(x * x, axis=-1, keepdims=True)
    return x * lax.rsqrt(ms + RMS_EPS) * g


def _sigmoid(x):
    return 1.0 / (1.0 + jnp.exp(-x))


def _halo(e_ref):
    return e_ref.shape[1] - GROUP


def _tail_index(prefix, rows, lanes, offset=0):
    return prefix + (slice(offset + rows.start, offset + rows.stop), lanes)


def _wrap_history(e_ref, q, prev_ref, prev_prefix=(), prev_offset=0, lanes=ALL):
    for rows in _blocks(_halo(e_ref), EW_ROWS):
        n = rows.stop - rows.start
        new = e_ref[_tail_index((q,), rows, lanes, GROUP)]
        prev = prev_ref[_tail_index(prev_prefix, rows, lanes, prev_offset)]
        shape3 = (n // SUBLANES, SUBLANES, new.shape[-1])
        sub = lax.broadcasted_iota(jnp.int32, shape3, 1)
        mixed = jnp.where(sub == SUBLANES - 1, prev.reshape(shape3), new.reshape(shape3))
        e_ref[q, rows, lanes] = pltpu.roll(mixed, 1, axis=1).reshape(n, new.shape[-1])


def _copy_tail(e_ref, q, dst_ref, dst_prefix=(), lanes=ALL):
    for rows in _blocks(_halo(e_ref), EW_ROWS):
        dst_ref[_tail_index(dst_prefix, rows, lanes)] = e_ref[_tail_index((q,), rows, lanes, GROUP)]


def _taps(e_ref, q, w_ref, rows, lanes):
    n_taps = w_ref.shape[0]
    acc = None
    for k in range(n_taps):
        start = _halo(e_ref) + rows.start - SUBLANES * (n_taps - 1 - k)
        term = w_ref[k:k + 1, lanes] * e_ref[q, start:start + rows.stop - rows.start, lanes]
        acc = term if acc is None else acc + term
    return acc


def _conv_columns(e_ref, q, w_ref, out_ref, lanes, delay_groups):
    halo = _halo(e_ref)
    n_taps = w_ref.shape[0]
    n_vregs = GROUP // SUBLANES
    for group_index, delays in enumerate(delay_groups):
        w = {d: w_ref[n_taps - 1 - d:n_taps - d, lanes] for d in delays}
        acc = {}
        for u in range(-max(delays), n_vregs):
            targets = [d for d in delays if 0 <= u + d < n_vregs]
            if not targets:
                continue
            z = e_ref[q, halo + SUBLANES * u:halo + SUBLANES * (u + 1), lanes]
            for d in targets:
                term = w[d] * z
                acc[u + d] = term if u + d not in acc else acc[u + d] + term
            done = u + min(delays)
            if 0 <= done < n_vregs:
                rows = slice(SUBLANES * done, SUBLANES * (done + 1))
                total = acc.pop(done)
                out_ref[rows, lanes] = total if group_index == 0 else total + out_ref[rows, lanes]


def _dot_into(dst_ref, dst_cols, lhs_ref, lhs_cols, rhs_ref, rhs_rows, rhs_cols, accumulate=False):
    out = jnp.dot(lhs_ref[:, lhs_cols], rhs_ref[rhs_rows, rhs_cols], preferred_element_type=jnp.float32)
    dst_ref[:, dst_cols] = out + dst_ref[:, dst_cols] if accumulate else out


def _mixer_kernel(x_ref, xprev_ref, gpre_ref, gpost_ref, win_ref, caw_ref, cbw_ref, cbb_ref, lng_ref,
                  lnb_ref, pw_ref, ps_ref, wout_ref, o_ref, h_scr, ua_scr, ub_scr, uc_scr, a_e, b_e,
                  c_e, a_tail, b_tail, c_tail, conv_scr, pool_scr, ycat, y_scr, *, groups_per_seq):
    wa = N_HEADS_A * HEAD_DIM
    wb = N_HEADS_B * HEAD_DIM
    wc = len(POOL_WINDOWS) * HEAD_DIM
    a_cols = wa // A_PIECES
    off_b = 3 * wa
    off_c = off_b + 2 * wb
    ycat_c, ycat_b = wa, wa + wc
    d_model = x_ref.shape[1]
    step = pl.program_id(0)
    seq_group = step % groups_per_seq
    halo_a, halo_b, halo_c = _halo(a_e), _halo(b_e), _halo(c_e)

    @pl.when(step == 0)
    def _():
        ycat[...] = jnp.zeros(ycat.shape, ycat.dtype)
        conv_scr[...] = jnp.zeros(conv_scr.shape, conv_scr.dtype)

    @pl.when(seq_group == 0)
    def _():
        for tail in (a_tail, b_tail, c_tail):
            tail[...] = jnp.zeros(tail.shape, jnp.float32)

    def pre_norm(rows):
        h_scr[rows, :] = _rms_norm(x_ref[rows, :], gpre_ref[...]).astype(jnp.bfloat16)

    def post_norm_previous(rows):
        o_ref[rows, :] = xprev_ref[rows, :] + _rms_norm(y_scr[rows, :], gpost_ref[...])

    def b_glu(rows):
        b_e[0, halo_b + rows.start:halo_b + rows.stop, :] = (
            ub_scr[rows, 0:wb] * _sigmoid(ub_scr[rows, wb:2 * wb]))

    def b_conv(head):
        k_b = cbw_ref.shape[0]
        halves = (range(0, k_b // 2 + 1), range(k_b // 2 + 1, k_b))
        _conv_columns(b_e, 0, cbw_ref, conv_scr, slice(head * HEAD_DIM, (head + 1) * HEAD_DIM), halves)

    def b_norm_previous(rows):
        conv = conv_scr[rows, :] + cbb_ref[...]
        mu = jnp.mean(conv, axis=-1, keepdims=True)
        cen = conv - mu
        var = jnp.mean(cen * cen, axis=-1, keepdims=True)
        ln = cen * lax.rsqrt(var + LN_EPS) * lng_ref[...] + lnb_ref[...]
        ycat[rows, ycat_b:ycat_b + wb] = (ln * _sigmoid(ln)).astype(jnp.bfloat16)

    def a_mix(piece):
        lanes = slice(piece * a_cols, (piece + 1) * a_cols)
        b0 = piece * 3 * a_cols
        for rows in _blocks(GROUP, EW_ROWS):
            a_e[0, halo_a + rows.start:halo_a + rows.stop, lanes] = (
                ua_scr[rows, b0 + a_cols:b0 + 2 * a_cols] * ua_scr[rows, b0 + 2 * a_cols:b0 + 3 * a_cols])
        _wrap_history(a_e, 0, a_tail, lanes=lanes)
        _copy_tail(a_e, 0, a_tail, lanes=lanes)
        for rows in _blocks(GROUP, EW_ROWS):
            conv = _taps(a_e, 0, caw_ref, rows, lanes)
            ycat[rows, lanes] = (ua_scr[rows, b0:b0 + a_cols] * conv).astype(jnp.bfloat16)

    def c_pool():
        for rows in _blocks(GROUP, 2 * EW_ROWS):
            c_e[0, halo_c + rows.start:halo_c + rows.stop, :] = uc_scr[rows, :]
        _wrap_history(c_e, 0, c_tail)
        _copy_tail(c_e, 0, c_tail)
        for g, window in enumerate(POOL_WINDOWS):
            lanes = slice(g * HEAD_DIM, (g + 1) * HEAD_DIM)
            for rows in _blocks(GROUP, 2 * EW_ROWS):
                n = rows.stop - rows.start
                base = halo_c + rows.start
                self_rows = c_e[0, base:base + n, lanes]
                total = self_rows
                for back in range(1, window):
                    total = total + c_e[0, base - SUBLANES * back:base - SUBLANES * back + n, lanes]
                row = rows.start + lax.broadcasted_iota(jnp.int32, (n, HEAD_DIM), 0)
                pos = seq_group * GROUP + (row % SUBLANES) * (GROUP // SUBLANES) + row // SUBLANES
                counts = jnp.minimum(pos + 1, window).astype(jnp.float32)
                pool_scr[rows, lanes] = (total / counts - self_rows).astype(jnp.bfloat16)

    norm_blocks = _blocks(GROUP, NORM_ROWS)
    ew_blocks = _blocks(GROUP, EW_ROWS)

    def spread(blocks, n_parts, part):
        per = -(-len(blocks) // n_parts)
        return blocks[part * per:(part + 1) * per]


    out_pieces = _blocks(d_model, OUT_COLS)
    for k, cols in enumerate(out_pieces):
        for rows in spread(ew_blocks, len(out_pieces), k):
            b_norm_previous(rows)
        _dot_into(y_scr, cols, ycat, slice(0, ycat_b), wout_ref, slice(0, ycat_b), cols)
    for k, cols in enumerate(out_pieces):
        for rows in spread(norm_blocks, len(out_pieces), k):
            pre_norm(rows)
        _dot_into(y_scr, cols, ycat, slice(ycat_b, ycat_b + wb), wout_ref, slice(ycat_b, ycat_b + wb),
                  cols, accumulate=True)

    b_pieces = _blocks(2 * wb, DOT_COLS)
    for k, cols in enumerate(b_pieces):
        for rows in spread(norm_blocks, len(b_pieces), k):
            post_norm_previous(rows)
        _dot_into(ub_scr, cols, h_scr, ALL, win_ref, ALL, slice(off_b + cols.start, off_b + cols.stop))

    def project_a(piece):
        cols = slice(piece * 3 * a_cols, (piece + 1) * 3 * a_cols)
        _dot_into(ua_scr, cols, h_scr, ALL, win_ref, ALL, cols)

    for rows in ew_blocks:
        b_glu(rows)
    _wrap_history(b_e, 0, b_tail)
    _copy_tail(b_e, 0, b_tail)
    _dot_into(uc_scr, ALL, h_scr, ALL, win_ref, ALL, slice(off_c, off_c + wc))
    c_pool()
    project_a(0)

    def stage(k):
        return pl.when(step > -1 - k)

    @stage(0)
    def _():
        heads_b = list(range(N_HEADS_B))
        a_mix(0)
        for piece in range(1, A_PIECES):
            for head in spread(heads_b, A_PIECES - 1, piece - 1):
                b_conv(head)
            project_a(piece)

    @stage(1)
    def _():
        for piece in range(1, A_PIECES):
            a_mix(piece)
        for g in range(len(POOL_WINDOWS)):
            lanes = slice(g * HEAD_DIM, (g + 1) * HEAD_DIM)
            mixed = jnp.dot(pool_scr[:, lanes], pw_ref[g], preferred_element_type=jnp.float32)
            ycat[:, ycat_c + g * HEAD_DIM:ycat_c + (g + 1) * HEAD_DIM] = (
                mixed * ps_ref[:, lanes]).astype(jnp.bfloat16)


def _mixer_layer(xs, layer, p, seq_len):
    n_tok, d_model = xs.shape
    assert seq_len % GROUP == 0 and n_tok % seq_len == 0
    wa = N_HEADS_A * HEAD_DIM
    wb = N_HEADS_B * HEAD_DIM
    wc = len(POOL_WINDOWS) * HEAD_DIM
    d_in = p["w_in"].shape[-1]
    assert d_in == 3 * wa + 2 * wb + wc and p["w_out"].shape[1] == wa + wb + wc
    assert d_model % OUT_COLS == 0 and (2 * wb) % DOT_COLS == 0
    k_a = p["conv_a_w"].shape[1]
    k_b = p["conv_b_w"].shape[1]
    halo_a = SUBLANES * (k_a - 1)
    halo_b = SUBLANES * (k_b - 1)
    halo_c = SUBLANES * (max(POOL_WINDOWS) - 1)
    assert max(halo_a, halo_b, halo_c) <= GROUP
    n_groups = n_tok // GROUP

    def layer_block(shape, **kw):
        zeros = (0,) * len(shape)
        return pl.BlockSpec((None,) + shape, lambda i: (layer,) + zeros, **kw)

    resident = dict(pipeline_mode=pl.Buffered(1))
    f32, bf16 = jnp.float32, jnp.bfloat16
    return pl.pallas_call(
        functools.partial(_mixer_kernel, groups_per_seq=seq_len // GROUP),
        grid=(n_groups + 1,),
        in_specs=[
            pl.BlockSpec((GROUP, d_model), lambda i: (jnp.minimum(i, n_groups - 1), 0)),
            pl.BlockSpec((GROUP, d_model), lambda i: (jnp.maximum(i - 1, 0), 0)),
            layer_block((1, d_model)),
            layer_block((1, d_model)),
            layer_block((d_model, d_in), **resident),
            layer_block((k_a, wa)),
            layer_block((k_b, wb)),
            layer_block((1, wb)),
            layer_block((1, wb)),
            layer_block((1, wb)),
            layer_block((len(POOL_WINDOWS), HEAD_DIM, HEAD_DIM)),
            layer_block((1, wc)),
            layer_block((wa + wb + wc, d_model), **resident),
        ],
        out_specs=pl.BlockSpec((GROUP, d_model), lambda i: (jnp.maximum(i - 1, 0), 0)),
        out_shape=jax.ShapeDtypeStruct((n_tok, d_model), f32),
        scratch_shapes=[
            pltpu.VMEM((GROUP, d_model), bf16),
            pltpu.VMEM((GROUP, 3 * wa), f32),
            pltpu.VMEM((GROUP, 2 * wb), f32),
            pltpu.VMEM((GROUP, wc), f32),
            pltpu.VMEM((1, halo_a + GROUP, wa), f32),
            pltpu.VMEM((1, halo_b + GROUP, wb), f32),
            pltpu.VMEM((1, halo_c + GROUP, wc), f32),
            pltpu.VMEM((halo_a, wa), f32),
            pltpu.VMEM((halo_b, wb), f32),
            pltpu.VMEM((halo_c, wc), f32),
            pltpu.VMEM((GROUP, wb), f32),
            pltpu.VMEM((GROUP, wc), bf16),
            pltpu.VMEM((GROUP, wa + wb + wc), bf16),
            pltpu.VMEM((GROUP, d_model), f32),
        ],
        compiler_params=pltpu.CompilerParams(
            dimension_semantics=("arbitrary",), vmem_limit_bytes=VMEM_LIMIT_BYTES),
        name="mixer",
    )(xs, xs, p["norm_mix_pre"], p["norm_mix_post"], p["w_in"], p["conv_a_w"], p["conv_b_w"],
      p["conv_b_bias"], p["ln_b_gain"], p["ln_b_bias"], p["pool_w"], p["pool_scale"], p["w_out"])


def _ffn_kernel(x_ref, gpre_ref, gpost_ref, wup_ref, cw_ref, cb_ref, wd_ref, o_ref, h_scr, up_e,
                act_scr, carry, *, tiles_per_seq):
    n_groups = up_e.shape[0]
    halo = _halo(up_e)
    chunk = wd_ref.shape[0]
    tile = x_ref.shape[0]
    i = pl.program_id(0)
    j = pl.program_id(1)

    @pl.when(j == 0)
    def _():
        for rows in _blocks(tile, NORM_ROWS):
            h_scr[rows, :] = _rms_norm(x_ref[rows, :], gpre_ref[...]).astype(jnp.bfloat16)
            o_ref[rows, :] = jnp.zeros((rows.stop - rows.start, o_ref.shape[1]), jnp.float32)

    @pl.when(i % tiles_per_seq == 0)
    def _():
        carry[j] = jnp.zeros(carry.shape[1:], jnp.float32)

    def group_rows(q):
        return slice(q * GROUP, (q + 1) * GROUP)

    def project_up(q):
        up_e[q, halo:halo + GROUP, :] = jnp.dot(h_scr[group_rows(q), :], wup_ref[...],
                                                 preferred_element_type=jnp.float32)

    def gate_group(q):
        if q == 0:
            _wrap_history(up_e, q, carry, (j,))
        else:
            _wrap_history(up_e, q, up_e, (q - 1,), GROUP)
        if q == n_groups - 1:
            _copy_tail(up_e, q, carry, (j,))
        for rows in _blocks(GROUP, EW_ROWS):
            for lanes in _blocks(chunk, 2 * LANES):
                v_lanes = slice(chunk + lanes.start, chunk + lanes.stop)
                gate = _taps(up_e, q, cw_ref, rows, lanes) + cb_ref[:, lanes]
                val = _taps(up_e, q, cw_ref, rows, v_lanes) + cb_ref[:, v_lanes]
                act_scr[q * GROUP + rows.start:q * GROUP + rows.stop, lanes] = (
                    gate * _sigmoid(gate) * val).astype(jnp.bfloat16)

    def project_down(q):
        rows = group_rows(q)
        o_ref[rows, :] = (jnp.dot(act_scr[rows, :], wd_ref[...], preferred_element_type=jnp.float32)
                          + o_ref[rows, :])

    matmuls = ([functools.partial(project_up, q) for q in range(n_groups)]
               + [functools.partial(project_down, q) for q in range(n_groups)])
    gate_before = {min(q + 2, n_groups + q): q for q in range(n_groups)}
    for position, matmul in enumerate(matmuls):
        if position in gate_before:
            gate_group(gate_before[position])
        matmul()

    @pl.when(j == pl.num_programs(1) - 1)
    def _():
        for rows in _blocks(tile, NORM_ROWS):
            o_ref[rows, :] = x_ref[rows, :] + _rms_norm(o_ref[rows, :], gpost_ref[...])


def _ffn_layer(xs, layer, p, seq_len, n_groups):
    n_tok, d_model = xs.shape
    tile = n_groups * GROUP
    n_chunks, chunk = p["w_down"].shape[1:3]
    assert seq_len % tile == 0 and n_tok % seq_len == 0
    k_f = p["conv_ffn_w"].shape[2]
    halo = SUBLANES * (k_f - 1)
    assert halo <= GROUP

    tok_spec = pl.BlockSpec((tile, d_model), lambda i, j: (i, 0))
    vec_spec = pl.BlockSpec((None, 1, d_model), lambda i, j: (layer, 0, 0))

    def chunk_block(rows, cols):
        return pl.BlockSpec((None, None, rows, cols), lambda i, j: (layer, j, 0, 0))

    return pl.pallas_call(
        functools.partial(_ffn_kernel, tiles_per_seq=seq_len // tile),
        grid=(n_tok // tile, n_chunks),
        in_specs=[
            tok_spec, vec_spec, vec_spec,
            chunk_block(d_model, 2 * chunk),
            chunk_block(k_f, 2 * chunk),
            chunk_block(1, 2 * chunk),
            chunk_block(chunk, d_model),
        ],
        out_specs=tok_spec,
        out_shape=jax.ShapeDtypeStruct((n_tok, d_model), jnp.float32),
        scratch_shapes=[
            pltpu.VMEM((tile, d_model), jnp.bfloat16),
            pltpu.VMEM((n_groups, halo + GROUP, 2 * chunk), jnp.float32),
            pltpu.VMEM((tile, chunk), jnp.bfloat16),
            pltpu.VMEM((n_chunks, halo, 2 * chunk), jnp.float32),
        ],
        compiler_params=pltpu.CompilerParams(
            dimension_semantics=("arbitrary", "arbitrary"), vmem_limit_bytes=VMEM_LIMIT_BYTES),
        name="ffn",
    )(xs, p["norm_ffn_pre"], p["norm_ffn_post"], p["w_up"], p["conv_ffn_w"], p["conv_ffn_bias"],
      p["w_down"])


def _chunk_major(a, chunk):
    depth, rows, two_d_ff = a.shape
    n_chunks = two_d_ff // (2 * chunk)
    a = a.reshape(depth, rows, 2, n_chunks, chunk)
    return a.transpose(0, 3, 1, 2, 4).reshape(depth, n_chunks, rows, 2 * chunk)


def _a_piece_major(w_in):
    depth, d_model, _ = w_in.shape
    wa = N_HEADS_A * HEAD_DIM
    a = w_in[:, :, :3 * wa].reshape(depth, d_model, 3, A_PIECES, wa // A_PIECES)
    a = a.transpose(0, 1, 3, 2, 4).reshape(depth, d_model, 3 * wa)
    return jnp.concatenate([a, w_in[:, :, 3 * wa:]], axis=-1)


def _prepare_params(params, ffn_chunk):
    p = dict(params)
    p["pool_w"] = params["pool_w"].astype(jnp.bfloat16)
    p["w_in"] = _a_piece_major(params["w_in"].astype(jnp.bfloat16))
    wa, wb = N_HEADS_A * HEAD_DIM, N_HEADS_B * HEAD_DIM
    w_out = params["w_out"].astype(jnp.bfloat16)
    p["w_out"] = jnp.concatenate([w_out[:, :wa], w_out[:, wa + wb:], w_out[:, wa:wa + wb]], axis=1)
    for name in ("norm_mix_pre", "norm_mix_post", "norm_ffn_pre", "norm_ffn_post", "conv_b_bias",
                 "ln_b_gain", "ln_b_bias", "pool_scale"):
        p[name] = params[name][:, None, :]
    depth, d_ff, d_model = params["w_down"].shape
    assert d_ff % ffn_chunk == 0 and params["w_up"].shape[-1] == 2 * d_ff
    p["w_up"] = _chunk_major(params["w_up"].astype(jnp.bfloat16), ffn_chunk)
    p["conv_ffn_w"] = _chunk_major(params["conv_ffn_w"], ffn_chunk)
    p["conv_ffn_bias"] = _chunk_major(params["conv_ffn_bias"][:, None, :], ffn_chunk)
    p["w_down"] = params["w_down"].astype(jnp.bfloat16).reshape(
        depth, d_ff // ffn_chunk, ffn_chunk, d_model)
    return p


def _to_stream_order(x2d):
    n_tok, d_model = x2d.shape
    per_sub = GROUP // SUBLANES
    return x2d.reshape(n_tok // GROUP, SUBLANES, per_sub, d_model).swapaxes(1, 2).reshape(n_tok, d_model)


def _from_stream_order(xs):
    n_tok, d_model = xs.shape
    per_sub = GROUP // SUBLANES
    return xs.reshape(n_tok // GROUP, per_sub, SUBLANES, d_model).swapaxes(1, 2).reshape(n_tok, d_model)


def _trunk(x, params, ffn_groups, ffn_chunk):
    batch, seq_len, d_model = x.shape
    p = _prepare_params(params, ffn_chunk)
    xs = _to_stream_order(x.reshape(batch * seq_len, d_model))
    for layer in range(params["w_in"].shape[0]):
        xs = _mixer_layer(xs, layer, p, seq_len)
        xs = _ffn_layer(xs, layer, p, seq_len, ffn_groups)
    return _from_stream_order(xs).reshape(batch, seq_len, d_model)


def kernel(x, norm_mix_pre, norm_mix_post, norm_ffn_pre, norm_ffn_post, w_in, conv_a_w, conv_b_w,
           conv_b_bias, ln_b_gain, ln_b_bias, pool_w, pool_scale, w_out, w_up, conv_ffn_w,
           conv_ffn_bias, w_down):
    params = dict(
        norm_mix_pre=norm_mix_pre, norm_mix_post=norm_mix_post, norm_ffn_pre=norm_ffn_pre,
        norm_ffn_post=norm_ffn_post, w_in=w_in, conv_a_w=conv_a_w, conv_b_w=conv_b_w,
        conv_b_bias=conv_b_bias, ln_b_gain=ln_b_gain, ln_b_bias=ln_b_bias, pool_w=pool_w,
        pool_scale=pool_scale, w_out=w_out, w_up=w_up, conv_ffn_w=conv_ffn_w,
        conv_ffn_bias=conv_ffn_bias, w_down=w_down)
    return _trunk(x, params, FFN_GROUPS, FFN_CHUNK)
```

```python
import functools

import jax
import jax.numpy as jnp
from jax import lax
from jax.experimental import pallas as pl
from jax.experimental.pallas import tpu as pltpu

HEAD_DIM = 128
N_HEADS_A = 6
N_HEADS_B = 6
POOL_WINDOWS = (2, 4, 8, 16)
RMS_EPS = 1e-6
LN_EPS = 1e-5

SUBLANES = 8
LANES = 128
V7X_VMEM_BYTES = 64 * 1024 * 1024
VMEM_LIMIT_BYTES = V7X_VMEM_BYTES - 6 * 1024 * 1024

GROUP = 256
FFN_GROUPS = 4
FFN_CHUNK = 512
NORM_ROWS = 16
EW_ROWS = 32
ALL = slice(None)


def _blocks(n, size):
    return [slice(s, min(s + size, n)) for s in range(0, n, size)]


def _rms_norm(x, g):
    ms = jnp.mean(x * x, axis=-1, keepdims=True)
    return x * lax.rsqrt(ms + RMS_EPS) * g


def _sigmoid(x):
    return 1.0 / (1.0 + jnp.exp(-x))


def _halo(e_ref):
    return e_ref.shape[1] - GROUP


def _tail_index(prefix, rows, lanes, offset=0):
    return prefix + (slice(offset + rows.start, offset + rows.stop), lanes)


def _wrap_history(e_ref, q, prev_ref, prev_prefix=(), prev_offset=0, lanes=ALL):
    for rows in _blocks(_halo(e_ref), EW_ROWS):
        n = rows.stop - rows.start
        new = e_ref[_tail_index((q,), rows, lanes, GROUP)]
        prev = prev_ref[_tail_index(prev_prefix, rows, lanes, prev_offset)]
        shape3 = (n // SUBLANES, SUBLANES, new.shape[-1])
        sub = lax.broadcasted_iota(jnp.int32, shape3, 1)
        mixed = jnp.where(sub == SUBLANES - 1, prev.reshape(shape3), new.reshape(shape3))
        e_ref[q, rows, lanes] = pltpu.roll(mixed, 1, axis=1).reshape(n, new.shape[-1])


def _copy_tail(e_ref, q, dst_ref, dst_prefix=(), lanes=ALL):
    for rows in _blocks(_halo(e_ref), EW_ROWS):
        dst_ref[_tail_index(dst_prefix, rows, lanes)] = e_ref[_tail_index((q,), rows, lanes, GROUP)]


def _taps(e_ref, q, w_ref, rows, lanes, w_lanes=None):
    w_lanes = lanes if w_lanes is None else w_lanes
    n_taps = w_ref.shape[0]
    acc = None
    for k in range(n_taps):
        start = _halo(e_ref) + rows.start - SUBLANES * (n_taps - 1 - k)
        term = w_ref[k:k + 1, w_lanes] * e_ref[q, start:start + rows.stop - rows.start, lanes]
        acc = term if acc is None else acc + term
    return acc


def _conv_columns(e_ref, q, w_ref, out_ref, lanes, delay_groups):
    halo = _halo(e_ref)
    n_taps = w_ref.shape[0]
    n_vregs = GROUP // SUBLANES
    for group_index, delays in enumerate(delay_groups):
        w = {d: w_ref[n_taps - 1 - d:n_taps - d, lanes] for d in delays}
        acc = {}
        for u in range(-max(delays), n_vregs):
            targets = [d for d in delays if 0 <= u + d < n_vregs]
            if not targets:
                continue
            z = e_ref[q, halo + SUBLANES * u:halo + SUBLANES * (u + 1), lanes]
            for d in targets:
                term = w[d] * z
                acc[u + d] = term if u + d not in acc else acc[u + d] + term
            done = u + min(delays)
            if 0 <= done < n_vregs:
                rows = slice(SUBLANES * done, SUBLANES * (done + 1))
                total = acc.pop(done)
                out_ref[rows, lanes] = total if group_index == 0 else total + out_ref[rows, lanes]


def _dot_into(dst_ref, lhs_ref, lhs_cols, rhs_ref, rhs_rows, rhs_cols, accumulate=False):
    out = jnp.dot(lhs_ref[:, lhs_cols], rhs_ref[rhs_rows, rhs_cols], preferred_element_type=jnp.float32)
    dst_ref[...] = out + dst_ref[...] if accumulate else out


def _mixer_kernel(x_ref, xprev_ref, gpre_ref, gpost_ref, win_ref, caw_ref, cbw_ref, cbb_ref, lng_ref,
                  lnb_ref, pw_ref, ps_ref, wout_ref, o_ref, h_scr, ua_scr, ub_scr, uc_scr, a_e, b_e,
                  c_e, a_tail, b_tail, c_tail, conv_scr, pool_scr, ycat, y_scr, *, groups_per_seq):
    wa = N_HEADS_A * HEAD_DIM
    wb = N_HEADS_B * HEAD_DIM
    wc = len(POOL_WINDOWS) * HEAD_DIM
    cols_a, cols_b, cols_c = slice(0, wa), slice(wa, wa + wb), slice(wa + wb, wa + wb + wc)
    off_b = 3 * wa
    off_c = off_b + 2 * wb
    step = pl.program_id(0)
    seq_group = step % groups_per_seq
    halo_a, halo_b, halo_c = _halo(a_e), _halo(b_e), _halo(c_e)

    @pl.when(step == 0)
    def _():
        for ref in (ycat, conv_scr, ua_scr, a_tail):
            ref[...] = jnp.zeros(ref.shape, ref.dtype)

    @pl.when((step + groups_per_seq - 1) % groups_per_seq == 0)
    def _():
        a_tail[...] = jnp.zeros(a_tail.shape, jnp.float32)

    @pl.when(seq_group == 0)
    def _():
        for tail in (b_tail, c_tail):
            tail[...] = jnp.zeros(tail.shape, jnp.float32)

    norm_blocks = _blocks(GROUP, NORM_ROWS)
    ew_blocks = _blocks(GROUP, EW_ROWS)

    _dot_into(y_scr, ycat, cols_c, wout_ref, cols_c, ALL)

    for rows in ew_blocks:
        a_e[0, halo_a + rows.start:halo_a + rows.stop, :] = (
            ua_scr[rows, wa:2 * wa] * ua_scr[rows, 2 * wa:3 * wa])
    _wrap_history(a_e, 0, a_tail)
    _copy_tail(a_e, 0, a_tail)
    for rows in ew_blocks:
        ycat[rows, cols_a] = (ua_scr[rows, 0:wa] * _taps(a_e, 0, caw_ref, rows, ALL)).astype(jnp.bfloat16)
    _dot_into(y_scr, ycat, cols_a, wout_ref, cols_a, ALL, accumulate=True)

    for rows in ew_blocks:
        conv = conv_scr[rows, :] + cbb_ref[...]
        mu = jnp.mean(conv, axis=-1, keepdims=True)
        cen = conv - mu
        var = jnp.mean(cen * cen, axis=-1, keepdims=True)
        ln = cen * lax.rsqrt(var + LN_EPS) * lng_ref[...] + lnb_ref[...]
        ycat[rows, cols_b] = (ln * _sigmoid(ln)).astype(jnp.bfloat16)
    _dot_into(y_scr, ycat, cols_b, wout_ref, cols_b, ALL, accumulate=True)

    for rows in norm_blocks:
        h_scr[rows, :] = _rms_norm(x_ref[rows, :], gpre_ref[...]).astype(jnp.bfloat16)
    _dot_into(ub_scr, h_scr, ALL, win_ref, ALL, slice(off_b, off_b + 2 * wb))

    for rows in norm_blocks:
        o_ref[rows, :] = xprev_ref[rows, :] + _rms_norm(y_scr[rows, :], gpost_ref[...])

    for rows in ew_blocks:
        b_e[0, halo_b + rows.start:halo_b + rows.stop, :] = (
            ub_scr[rows, 0:wb] * _sigmoid(ub_scr[rows, wb:2 * wb]))
    _wrap_history(b_e, 0, b_tail)
    _copy_tail(b_e, 0, b_tail)
    _dot_into(uc_scr, h_scr, ALL, win_ref, ALL, slice(off_c, off_c + wc))

    def c_pool():
        for rows in _blocks(GROUP, 2 * EW_ROWS):
            c_e[0, halo_c + rows.start:halo_c + rows.stop, :] = uc_scr[rows, :]
        _wrap_history(c_e, 0, c_tail)
        _copy_tail(c_e, 0, c_tail)
        for g, window in enumerate(POOL_WINDOWS):
            lanes = slice(g * HEAD_DIM, (g + 1) * HEAD_DIM)
            for rows in _blocks(GROUP, 2 * EW_ROWS):
                n = rows.stop - rows.start
                base = halo_c + rows.start
                self_rows = c_e[0, base:base + n, lanes]
                total = self_rows
                for back in range(1, window):
                    total = total + c_e[0, base - SUBLANES * back:base - SUBLANES * back + n, lanes]
                row = rows.start + lax.broadcasted_iota(jnp.int32, (n, HEAD_DIM), 0)
                pos = seq_group * GROUP + (row % SUBLANES) * (GROUP // SUBLANES) + row // SUBLANES
                counts = jnp.minimum(pos + 1, window).astype(jnp.float32)
                pool_scr[rows, lanes] = (total / counts - self_rows).astype(jnp.bfloat16)

    @pl.when(step >= 0)
    def _():
        c_pool()
        k_b = cbw_ref.shape[0]
        halves = (range(0, k_b // 2 + 1), range(k_b // 2 + 1, k_b))
        heads_per_part = N_HEADS_B // 3
        for part in range(3):
            for head in range(part * heads_per_part, (part + 1) * heads_per_part):
                _conv_columns(b_e, 0, cbw_ref, conv_scr, slice(head * HEAD_DIM, (head + 1) * HEAD_DIM),
                              halves)
            cols = slice(part * wa, (part + 1) * wa)
            ua_scr[:, cols] = jnp.dot(h_scr[...], win_ref[:, cols], preferred_element_type=jnp.float32)
        for g in range(len(POOL_WINDOWS)):
            lanes = slice(g * HEAD_DIM, (g + 1) * HEAD_DIM)
            mixed = jnp.dot(pool_scr[:, lanes], pw_ref[g], preferred_element_type=jnp.float32)
            ycat[:, cols_c.start + lanes.start:cols_c.start + lanes.stop] = (
                mixed * ps_ref[:, lanes]).astype(jnp.bfloat16)


def _mixer_layer(xs, layer, p, seq_len):
    n_tok, d_model = xs.shape
    assert seq_len % GROUP == 0 and n_tok % seq_len == 0
    wa = N_HEADS_A * HEAD_DIM
    wb = N_HEADS_B * HEAD_DIM
    wc = len(POOL_WINDOWS) * HEAD_DIM
    d_in = p["w_in"].shape[-1]
    assert d_in == 3 * wa + 2 * wb + wc and p["w_out"].shape[1] == wa + wb + wc
    k_a = p["conv_a_w"].shape[1]
    k_b = p["conv_b_w"].shape[1]
    halo_a = SUBLANES * (k_a - 1)
    halo_b = SUBLANES * (k_b - 1)
    halo_c = SUBLANES * (max(POOL_WINDOWS) - 1)
    assert max(halo_a, halo_b, halo_c) <= GROUP
    n_groups = n_tok // GROUP

    def layer_block(shape, **kw):
        zeros = (0,) * len(shape)
        return pl.BlockSpec((None,) + shape, lambda i: (layer,) + zeros, **kw)

    resident = dict(pipeline_mode=pl.Buffered(1))
    f32, bf16 = jnp.float32, jnp.bfloat16
    return pl.pallas_call(
        functools.partial(_mixer_kernel, groups_per_seq=seq_len // GROUP),
        grid=(n_groups + 1,),
        in_specs=[
            pl.BlockSpec((GROUP, d_model), lambda i: (jnp.minimum(i, n_groups - 1), 0)),
            pl.BlockSpec((GROUP, d_model), lambda i: (jnp.maximum(i - 1, 0), 0)),
            layer_block((1, d_model)),
            layer_block((1, d_model)),
            layer_block((d_model, d_in), **resident),
            layer_block((k_a, wa)),
            layer_block((k_b, wb)),
            layer_block((1, wb)),
            layer_block((1, wb)),
            layer_block((1, wb)),
            layer_block((len(POOL_WINDOWS), HEAD_DIM, HEAD_DIM)),
            layer_block((1, wc)),
            layer_block((wa + wb + wc, d_model), **resident),
        ],
        out_specs=pl.BlockSpec((GROUP, d_model), lambda i: (jnp.maximum(i - 1, 0), 0)),
        out_shape=jax.ShapeDtypeStruct((n_tok, d_model), f32),
        scratch_shapes=[
            pltpu.VMEM((GROUP, d_model), bf16),
            pltpu.VMEM((GROUP, 3 * wa), f32),
            pltpu.VMEM((GROUP, 2 * wb), f32),
            pltpu.VMEM((GROUP, wc), f32),
            pltpu.VMEM((1, halo_a + GROUP, wa), f32),
            pltpu.VMEM((1, halo_b + GROUP, wb), f32),
            pltpu.VMEM((1, halo_c + GROUP, wc), f32),
            pltpu.VMEM((halo_a, wa), f32),
            pltpu.VMEM((halo_b, wb), f32),
            pltpu.VMEM((halo_c, wc), f32),
            pltpu.VMEM((GROUP, wb), f32),
            pltpu.VMEM((GROUP, wc), bf16),
            pltpu.VMEM((GROUP, wa + wb + wc), bf16),
            pltpu.VMEM((GROUP, d_model), f32),
        ],
        compiler_params=pltpu.CompilerParams(
            dimension_semantics=("arbitrary",), vmem_limit_bytes=VMEM_LIMIT_BYTES),
        name="mixer",
    )(xs, xs, p["norm_mix_pre"], p["norm_mix_post"], p["w_in"], p["conv_a_w"], p["conv_b_w"],
      p["conv_b_bias"], p["ln_b_gain"], p["ln_b_bias"], p["pool_w"], p["pool_scale"], p["w_out"])


def _ffn_kernel(x_ref, gpre_ref, gpost_ref, wg_ref, wv_ref, cwg_ref, cwv_ref, cbg_ref, cbv_ref, wd_ref,
                o_ref, h_scr, up_e, act_scr, carry, *, tiles_per_seq):
    n_groups = up_e.shape[0]
    halo = _halo(up_e)
    chunk = wd_ref.shape[0]
    tile = x_ref.shape[0]
    i = pl.program_id(0)
    j = pl.program_id(1)

    @pl.when(j == 0)
    def _():
        for rows in _blocks(tile, NORM_ROWS):
            h_scr[rows, :] = _rms_norm(x_ref[rows, :], gpre_ref[...]).astype(jnp.bfloat16)
            o_ref[rows, :] = jnp.zeros((rows.stop - rows.start, o_ref.shape[1]), jnp.float32)

    @pl.when(i % tiles_per_seq == 0)
    def _():
        carry[j] = jnp.zeros(carry.shape[1:], jnp.float32)

    def group_rows(q):
        return slice(q * GROUP, (q + 1) * GROUP)

    def project_up(w_ref, half):
        up = jnp.dot(h_scr[...], w_ref[...], preferred_element_type=jnp.float32)
        for q in range(n_groups):
            up_e[q, halo:halo + GROUP, half * chunk:(half + 1) * chunk] = up[group_rows(q), :]

    def gate_group(q):
        if q == 0:
            _wrap_history(up_e, q, carry, (j,))
        else:
            _wrap_history(up_e, q, up_e, (q - 1,), GROUP)
        if q == n_groups - 1:
            _copy_tail(up_e, q, carry, (j,))
        for rows in _blocks(GROUP, EW_ROWS):
            for lanes in _blocks(chunk, 2 * LANES):
                v_lanes = slice(chunk + lanes.start, chunk + lanes.stop)
                gate = _taps(up_e, q, cwg_ref, rows, lanes) + cbg_ref[:, lanes]
                val = _taps(up_e, q, cwv_ref, rows, v_lanes, lanes) + cbv_ref[:, lanes]
                act_scr[q * GROUP + rows.start:q * GROUP + rows.stop, lanes] = (
                    gate * _sigmoid(gate) * val).astype(jnp.bfloat16)

    def project_down(q):
        rows = group_rows(q)
        o_ref[rows, :] = (jnp.dot(act_scr[rows, :], wd_ref[...], preferred_element_type=jnp.float32)
                          + o_ref[rows, :])

    project_up(wg_ref, 0)
    project_up(wv_ref, 1)
    for q in range(n_groups):
        gate_group(q)
        project_down(q)

    @pl.when(j == pl.num_programs(1) - 1)
    def _():
        for rows in _blocks(tile, NORM_ROWS):
            o_ref[rows, :] = x_ref[rows, :] + _rms_norm(o_ref[rows, :], gpost_ref[...])


def _ffn_layer(xs, layer, p, seq_len, n_groups, chunk):
    n_tok, d_model = xs.shape
    tile = n_groups * GROUP
    d_ff = p["w_down"].shape[1]
    assert seq_len % tile == 0 and n_tok % seq_len == 0 and d_ff % chunk == 0
    assert p["w_up"].shape[-1] == 2 * d_ff
    n_chunks = d_ff // chunk
    k_f = p["conv_ffn_w"].shape[1]
    halo = SUBLANES * (k_f - 1)
    assert halo <= GROUP

    tok_spec = pl.BlockSpec((tile, d_model), lambda i, j: (i, 0))
    vec_spec = pl.BlockSpec((None, 1, d_model), lambda i, j: (layer, 0, 0))

    def up_cols(rows, half):
        return pl.BlockSpec((None, rows, chunk), lambda i, j: (layer, 0, half * n_chunks + j))

    return pl.pallas_call(
        functools.partial(_ffn_kernel, tiles_per_seq=seq_len // tile),
        grid=(n_tok // tile, n_chunks),
        in_specs=[
            tok_spec, vec_spec, vec_spec,
            up_cols(d_model, 0), up_cols(d_model, 1),
            up_cols(k_f, 0), up_cols(k_f, 1),
            up_cols(1, 0), up_cols(1, 1),
            pl.BlockSpec((None, chunk, d_model), lambda i, j: (layer, j, 0)),
        ],
        out_specs=tok_spec,
        out_shape=jax.ShapeDtypeStruct((n_tok, d_model), jnp.float32),
        scratch_shapes=[
            pltpu.VMEM((tile, d_model), jnp.bfloat16),
            pltpu.VMEM((n_groups, halo + GROUP, 2 * chunk), jnp.float32),
            pltpu.VMEM((tile, chunk), jnp.bfloat16),
            pltpu.VMEM((n_chunks, halo, 2 * chunk), jnp.float32),
        ],
        compiler_params=pltpu.CompilerParams(
            dimension_semantics=("arbitrary", "arbitrary"), vmem_limit_bytes=VMEM_LIMIT_BYTES),
        name="ffn",
    )(xs, p["norm_ffn_pre"], p["norm_ffn_post"], p["w_up"], p["w_up"], p["conv_ffn_w"],
      p["conv_ffn_w"], p["conv_ffn_bias"], p["conv_ffn_bias"], p["w_down"])


def _prepare_params(params):
    p = dict(params)
    for name in ("w_in", "w_out", "w_up", "w_down", "pool_w"):
        p[name] = params[name].astype(jnp.bfloat16)
    for name in ("norm_mix_pre", "norm_mix_post", "norm_ffn_pre", "norm_ffn_post", "conv_b_bias",
                 "ln_b_gain", "ln_b_bias", "pool_scale", "conv_ffn_bias"):
        p[name] = params[name][:, None, :]
    return p


def _to_stream_order(x2d):
    n_tok, d_model = x2d.shape
    per_sub = GROUP // SUBLANES
    return x2d.reshape(n_tok // GROUP, SUBLANES, per_sub, d_model).swapaxes(1, 2).reshape(n_tok, d_model)


def _from_stream_order(xs):
    n_tok, d_model = xs.shape
    per_sub = GROUP // SUBLANES
    return xs.reshape(n_tok // GROUP, per_sub, SUBLANES, d_model).swapaxes(1, 2).reshape(n_tok, d_model)


def _trunk(x, params, ffn_groups, ffn_chunk):
    batch, seq_len, d_model = x.shape
    p = _prepare_params(params)
    xs = _to_stream_order(x.reshape(batch * seq_len, d_model))
    for layer in range(params["w_in"].shape[0]):
        xs = _mixer_layer(xs, layer, p, seq_len)
        xs = _ffn_layer(xs, layer, p, seq_len, ffn_groups, ffn_chunk)
    return _from_stream_order(xs).reshape(batch, seq_len, d_model)


def kernel(x, norm_mix_pre, norm_mix_post, norm_ffn_pre, norm_ffn_post, w_in, conv_a_w, conv_b_w,
           conv_b_bias, ln_b_gain, ln_b_bias, pool_w, pool_scale, w_out, w_up, conv_ffn_w,
           conv_ffn_bias, w_down):
    params = dict(
        norm_mix_pre=norm_mix_pre, norm_mix_post=norm_mix_post, norm_ffn_pre=norm_ffn_pre,
        norm_ffn_post=norm_ffn_post, w_in=w_in, conv_a_w=conv_a_w, conv_b_w=conv_b_w,
        conv_b_bias=conv_b_bias, ln_b_gain=ln_b_gain, ln_b_bias=ln_b_bias, pool_w=pool_w,
        pool_scale=pool_scale, w_out=w_out, w_up=w_up, conv_ffn_w=conv_ffn_w,
        conv_ffn_bias=conv_ffn_bias, w_down=w_down)
    return _trunk(x, params, FFN_GROUPS, FFN_CHUNK)
```

```python
import functools

import jax
import jax.numpy as jnp
from jax import lax
from jax.experimental import pallas as pl
from jax.experimental.pallas import tpu as pltpu

HEAD_DIM = 128
N_HEADS_A = 6
N_HEADS_B = 6
POOL_WINDOWS = (2, 4, 8, 16)
RMS_EPS = 1e-6
LN_EPS = 1e-5

SUBLANES = 8
LANES = 128
V7X_VMEM_BYTES = 64 * 1024 * 1024
VMEM_LIMIT_BYTES = V7X_VMEM_BYTES - 6 * 1024 * 1024

GROUP = 256
FFN_GROUPS = 4
FFN_CHUNK = 512
NORM_ROWS = 16
EW_ROWS = 32
ALL = slice(None)


def _blocks(n, size):
    return [slice(s, min(s + size, n)) for s in range(0, n, size)]


def _rms_norm(x, g):
    ms = jnp.mean(x * x, axis=-1, keepdims=True)
    return x * lax.rsqrt(ms + RMS_EPS) * g


def _sigmoid(x):
    return 1.0 / (1.0 + jnp.exp(-x))


def _halo(e_ref):
    return e_ref.shape[1] - GROUP


def _tail_index(prefix, rows, lanes, offset=0):
    return prefix + (slice(offset + rows.start, offset + rows.stop), lanes)


def _wrap_history(e_ref, q, prev_ref, prev_prefix=(), prev_offset=0, lanes=ALL):
    for rows in _blocks(_halo(e_ref), EW_ROWS):
        n = rows.stop - rows.start
        new = e_ref[_tail_index((q,), rows, lanes, GROUP)]
        prev = prev_ref[_tail_index(prev_prefix, rows, lanes, prev_offset)]
        shape3 = (n // SUBLANES, SUBLANES, new.shape[-1])
        sub = lax.broadcasted_iota(jnp.int32, shape3, 1)
        mixed = jnp.where(sub == SUBLANES - 1, prev.reshape(shape3), new.reshape(shape3))
        e_ref[q, rows, lanes] = pltpu.roll(mixed, 1, axis=1).reshape(n, new.shape[-1])


def _copy_tail(e_ref, q, dst_ref, dst_prefix=(), lanes=ALL):
    for rows in _blocks(_halo(e_ref), EW_ROWS):
        dst_ref[_tail_index(dst_prefix, rows, lanes)] = e_ref[_tail_index((q,), rows, lanes, GROUP)]


def _taps(e_ref, q, w_ref, rows, lanes, w_lanes=None):
    w_lanes = lanes if w_lanes is None else w_lanes
    n_taps = w_ref.shape[0]
    acc = None
    for k in range(n_taps):
        start = _halo(e_ref) + rows.start - SUBLANES * (n_taps - 1 - k)
        term = w_ref[k:k + 1, w_lanes] * e_ref[q, start:start + rows.stop - rows.start, lanes]
        acc = term if acc is None else acc + term
    return acc


def _conv_columns(e_ref, q, w_ref, out_ref, lanes, delay_groups):
    halo = _halo(e_ref)
    n_taps = w_ref.shape[0]
    n_vregs = GROUP // SUBLANES
    for group_index, delays in enumerate(delay_groups):
        w = {d: w_ref[n_taps - 1 - d:n_taps - d, lanes] for d in delays}
        acc = {}
        for u in range(-max(delays), n_vregs):
            targets = [d for d in delays if 0 <= u + d < n_vregs]
            if not targets:
                continue
            z = e_ref[q, halo + SUBLANES * u:halo + SUBLANES * (u + 1), lanes]
            for d in targets:
                term = w[d] * z
                acc[u + d] = term if u + d not in acc else acc[u + d] + term
            done = u + min(delays)
            if 0 <= done < n_vregs:
                rows = slice(SUBLANES * done, SUBLANES * (done + 1))
                total = acc.pop(done)
                out_ref[rows, lanes] = total if group_index == 0 else total + out_ref[rows, lanes]


def _dot_into(dst_ref, lhs_ref, lhs_cols, rhs_ref, rhs_rows, rhs_cols, accumulate=False):
    out = jnp.dot(lhs_ref[:, lhs_cols], rhs_ref[rhs_rows, rhs_cols], preferred_element_type=jnp.float32)
    dst_ref[...] = out + dst_ref[...] if accumulate else out


def _mixer_kernel(x_ref, xprev_ref, gpre_ref, gpost_ref, win_ref, caw_ref, cbw_ref, cbb_ref, lng_ref,
                  lnb_ref, pw_ref, ps_ref, wout_ref, o_ref, h_scr, ua_scr, ub_scr, uc_scr, a_e, b_e,
                  c_e, a_tail, b_tail, c_tail, conv_scr, pool_scr, ycat, y_scr, *, groups_per_seq):
    wa = N_HEADS_A * HEAD_DIM
    wb = N_HEADS_B * HEAD_DIM
    wc = len(POOL_WINDOWS) * HEAD_DIM
    cols_a, cols_b, cols_c = slice(0, wa), slice(wa, wa + wb), slice(wa + wb, wa + wb + wc)
    off_b = 3 * wa
    off_c = off_b + 2 * wb
    step = pl.program_id(0)
    seq_group = step % groups_per_seq
    halo_a, halo_b, halo_c = _halo(a_e), _halo(b_e), _halo(c_e)

    @pl.when(step == 0)
    def _():
        for ref in (ycat, conv_scr, ua_scr, a_tail):
            ref[...] = jnp.zeros(ref.shape, ref.dtype)

    @pl.when((step + groups_per_seq - 1) % groups_per_seq == 0)
    def _():
        a_tail[...] = jnp.zeros(a_tail.shape, jnp.float32)

    @pl.when(seq_group == 0)
    def _():
        for tail in (b_tail, c_tail):
            tail[...] = jnp.zeros(tail.shape, jnp.float32)

    norm_blocks = _blocks(GROUP, NORM_ROWS)
    ew_blocks = _blocks(GROUP, EW_ROWS)

    _dot_into(y_scr, ycat, cols_c, wout_ref, cols_c, ALL)

    for rows in ew_blocks:
        a_e[0, halo_a + rows.start:halo_a + rows.stop, :] = (
            ua_scr[rows, wa:2 * wa] * ua_scr[rows, 2 * wa:3 * wa])
    _wrap_history(a_e, 0, a_tail)
    _copy_tail(a_e, 0, a_tail)
    for rows in ew_blocks:
        ycat[rows, cols_a] = (ua_scr[rows, 0:wa] * _taps(a_e, 0, caw_ref, rows, ALL)).astype(jnp.bfloat16)
    _dot_into(y_scr, ycat, cols_a, wout_ref, cols_a, ALL, accumulate=True)

    for rows in ew_blocks:
        conv = conv_scr[rows, :] + cbb_ref[...]
        mu = jnp.mean(conv, axis=-1, keepdims=True)
        cen = conv - mu
        var = jnp.mean(cen * cen, axis=-1, keepdims=True)
        ln = cen * lax.rsqrt(var + LN_EPS) * lng_ref[...] + lnb_ref[...]
        ycat[rows, cols_b] = (ln * _sigmoid(ln)).astype(jnp.bfloat16)
    _dot_into(y_scr, ycat, cols_b, wout_ref, cols_b, ALL, accumulate=True)

    for rows in norm_blocks:
        h_scr[rows, :] = _rms_norm(x_ref[rows, :], gpre_ref[...]).astype(jnp.bfloat16)
    _dot_into(ub_scr, h_scr, ALL, win_ref, ALL, slice(off_b, off_b + 2 * wb))

    for rows in norm_blocks:
        o_ref[rows, :] = xprev_ref[rows, :] + _rms_norm(y_scr[rows, :], gpost_ref[...])

    for rows in ew_blocks:
        b_e[0, halo_b + rows.start:halo_b + rows.stop, :] = (
            ub_scr[rows, 0:wb] * _sigmoid(ub_scr[rows, wb:2 * wb]))
    _wrap_history(b_e, 0, b_tail)
    _copy_tail(b_e, 0, b_tail)
    _dot_into(uc_scr, h_scr, ALL, win_ref, ALL, slice(off_c, off_c + wc))

    def c_pool():
        for rows in _blocks(GROUP, 2 * EW_ROWS):
            c_e[0, halo_c + rows.start:halo_c + rows.stop, :] = uc_scr[rows, :]
        _wrap_history(c_e, 0, c_tail)
        _copy_tail(c_e, 0, c_tail)
        for g, window in enumerate(POOL_WINDOWS):
            lanes = slice(g * HEAD_DIM, (g + 1) * HEAD_DIM)
            for rows in _blocks(GROUP, 2 * EW_ROWS):
                n = rows.stop - rows.start
                base = halo_c + rows.start
                self_rows = c_e[0, base:base + n, lanes]
                total = self_rows
                for back in range(1, window):
                    total = total + c_e[0, base - SUBLANES * back:base - SUBLANES * back + n, lanes]
                row = rows.start + lax.broadcasted_iota(jnp.int32, (n, HEAD_DIM), 0)
                pos = seq_group * GROUP + (row % SUBLANES) * (GROUP // SUBLANES) + row // SUBLANES
                counts = jnp.minimum(pos + 1, window).astype(jnp.float32)
                pool_scr[rows, lanes] = (total / counts - self_rows).astype(jnp.bfloat16)

    @pl.when(step >= 0)
    def _():
        c_pool()
        k_b = cbw_ref.shape[0]
        halves = (range(0, k_b // 2 + 1), range(k_b // 2 + 1, k_b))
        heads_per_part = N_HEADS_B // 3
        for part in range(3):
            for head in range(part * heads_per_part, (part + 1) * heads_per_part):
                _conv_columns(b_e, 0, cbw_ref, conv_scr, slice(head * HEAD_DIM, (head + 1) * HEAD_DIM),
                              halves)
            cols = slice(part * wa, (part + 1) * wa)
            ua_scr[:, cols] = jnp.dot(h_scr[...], win_ref[:, cols], preferred_element_type=jnp.float32)
        mixed = jnp.dot(pool_scr[...], pw_ref[...], preferred_element_type=jnp.float32)
        ycat[:, cols_c] = (mixed * ps_ref[...]).astype(jnp.bfloat16)


def _mixer_layer(xs, layer, p, seq_len):
    n_tok, d_model = xs.shape
    assert seq_len % GROUP == 0 and n_tok % seq_len == 0
    wa = N_HEADS_A * HEAD_DIM
    wb = N_HEADS_B * HEAD_DIM
    wc = len(POOL_WINDOWS) * HEAD_DIM
    d_in = p["w_in"].shape[-1]
    assert d_in == 3 * wa + 2 * wb + wc and p["w_out"].shape[1] == wa + wb + wc
    k_a = p["conv_a_w"].shape[1]
    k_b = p["conv_b_w"].shape[1]
    halo_a = SUBLANES * (k_a - 1)
    halo_b = SUBLANES * (k_b - 1)
    halo_c = SUBLANES * (max(POOL_WINDOWS) - 1)
    assert max(halo_a, halo_b, halo_c) <= GROUP
    n_groups = n_tok // GROUP

    def layer_block(shape, **kw):
        zeros = (0,) * len(shape)
        return pl.BlockSpec((None,) + shape, lambda i: (layer,) + zeros, **kw)

    resident = dict(pipeline_mode=pl.Buffered(1))
    f32, bf16 = jnp.float32, jnp.bfloat16
    return pl.pallas_call(
        functools.partial(_mixer_kernel, groups_per_seq=seq_len // GROUP),
        grid=(n_groups + 1,),
        in_specs=[
            pl.BlockSpec((GROUP, d_model), lambda i: (jnp.minimum(i, n_groups - 1), 0)),
            pl.BlockSpec((GROUP, d_model), lambda i: (jnp.maximum(i - 1, 0), 0)),
            layer_block((1, d_model)),
            layer_block((1, d_model)),
            layer_block((d_model, d_in), **resident),
            layer_block((k_a, wa)),
            layer_block((k_b, wb)),
            layer_block((1, wb)),
            layer_block((1, wb)),
            layer_block((1, wb)),
            layer_block((wc, wc)),
            layer_block((1, wc)),
            layer_block((wa + wb + wc, d_model), **resident),
        ],
        out_specs=pl.BlockSpec((GROUP, d_model), lambda i: (jnp.maximum(i - 1, 0), 0)),
        out_shape=jax.ShapeDtypeStruct((n_tok, d_model), f32),
        scratch_shapes=[
            pltpu.VMEM((GROUP, d_model), bf16),
            pltpu.VMEM((GROUP, 3 * wa), f32),
            pltpu.VMEM((GROUP, 2 * wb), f32),
            pltpu.VMEM((GROUP, wc), f32),
            pltpu.VMEM((1, halo_a + GROUP, wa), f32),
            pltpu.VMEM((1, halo_b + GROUP, wb), f32),
            pltpu.VMEM((1, halo_c + GROUP, wc), f32),
            pltpu.VMEM((halo_a, wa), f32),
            pltpu.VMEM((halo_b, wb), f32),
            pltpu.VMEM((halo_c, wc), f32),
            pltpu.VMEM((GROUP, wb), f32),
            pltpu.VMEM((GROUP, wc), bf16),
            pltpu.VMEM((GROUP, wa + wb + wc), bf16),
            pltpu.VMEM((GROUP, d_model), f32),
        ],
        compiler_params=pltpu.CompilerParams(
            dimension_semantics=("arbitrary",), vmem_limit_bytes=VMEM_LIMIT_BYTES),
        name="mixer",
    )(xs, xs, p["norm_mix_pre"], p["norm_mix_post"], p["w_in"], p["conv_a_w"], p["conv_b_w"],
      p["conv_b_bias"], p["ln_b_gain"], p["ln_b_bias"], p["pool_w"], p["pool_scale"], p["w_out"])


def _ffn_kernel(x_ref, gpre_ref, gpost_ref, wg_ref, wv_ref, cwg_ref, cwv_ref, cbg_ref, cbv_ref, wd_ref,
                o_ref, h_scr, up_e, silu_scr, act_scr, carry, *, tiles_per_seq):
    n_groups = up_e.shape[0]
    halo = _halo(up_e)
    chunk = wd_ref.shape[0]
    tile = x_ref.shape[0]
    i = pl.program_id(0)
    j = pl.program_id(1)

    @pl.when(j == 0)
    def _():
        for rows in _blocks(tile, NORM_ROWS):
            h_scr[rows, :] = _rms_norm(x_ref[rows, :], gpre_ref[...]).astype(jnp.bfloat16)
            o_ref[rows, :] = jnp.zeros((rows.stop - rows.start, o_ref.shape[1]), jnp.float32)

    @pl.when(i % tiles_per_seq == 0)
    def _():
        carry[j] = jnp.zeros(carry.shape[1:], jnp.float32)

    def group_rows(q):
        return slice(q * GROUP, (q + 1) * GROUP)

    def project_up(w_ref, half):
        up = jnp.dot(h_scr[...], w_ref[...], preferred_element_type=jnp.float32)
        for q in range(n_groups):
            up_e[q, halo:halo + GROUP, half * chunk:(half + 1) * chunk] = up[group_rows(q), :]

    def wrap_group(q, lanes):
        if q == 0:
            _wrap_history(up_e, q, carry, (j,), lanes=lanes)
        else:
            _wrap_history(up_e, q, up_e, (q - 1,), GROUP, lanes=lanes)

    def gate_part(q):
        wrap_group(q, slice(0, chunk))
        for rows in _blocks(GROUP, EW_ROWS):
            for lanes in _blocks(chunk, 2 * LANES):
                gate = _taps(up_e, q, cwg_ref, rows, lanes) + cbg_ref[:, lanes]
                silu_scr[q * GROUP + rows.start:q * GROUP + rows.stop, lanes] = gate * _sigmoid(gate)

    def value_part(q):
        wrap_group(q, slice(chunk, 2 * chunk))
        for rows in _blocks(GROUP, EW_ROWS):
            t_rows = slice(q * GROUP + rows.start, q * GROUP + rows.stop)
            for lanes in _blocks(chunk, 2 * LANES):
                v_lanes = slice(chunk + lanes.start, chunk + lanes.stop)
                val = _taps(up_e, q, cwv_ref, rows, v_lanes, lanes) + cbv_ref[:, lanes]
                act_scr[t_rows, lanes] = (silu_scr[t_rows, lanes] * val).astype(jnp.bfloat16)

    def project_down(rows):
        o_ref[rows, :] = (jnp.dot(act_scr[rows, :], wd_ref[...], preferred_element_type=jnp.float32)
                          + o_ref[rows, :])

    project_up(wg_ref, 0)
    for q in range(n_groups):
        gate_part(q)
    project_up(wv_ref, 1)
    halves = [range(0, (n_groups + 1) // 2), range((n_groups + 1) // 2, n_groups)]
    for half in halves:
        if len(half):
            for q in half:
                value_part(q)
            project_down(slice(half[0] * GROUP, (half[-1] + 1) * GROUP))
    _copy_tail(up_e, n_groups - 1, carry, (j,))

    @pl.when(j == pl.num_programs(1) - 1)
    def _():
        for rows in _blocks(tile, NORM_ROWS):
            o_ref[rows, :] = x_ref[rows, :] + _rms_norm(o_ref[rows, :], gpost_ref[...])


def _ffn_layer(xs, layer, p, seq_len, n_groups, chunk):
    n_tok, d_model = xs.shape
    tile = n_groups * GROUP
    d_ff = p["w_down"].shape[1]
    assert seq_len % tile == 0 and n_tok % seq_len == 0 and d_ff % chunk == 0
    assert p["w_up"].shape[-1] == 2 * d_ff
    n_chunks = d_ff // chunk
    k_f = p["conv_ffn_w"].shape[1]
    halo = SUBLANES * (k_f - 1)
    assert halo <= GROUP

    tok_spec = pl.BlockSpec((tile, d_model), lambda i, j: (i, 0))
    vec_spec = pl.BlockSpec((None, 1, d_model), lambda i, j: (layer, 0, 0))

    def up_cols(rows, half):
        return pl.BlockSpec((None, rows, chunk), lambda i, j: (layer, 0, half * n_chunks + j))

    return pl.pallas_call(
        functools.partial(_ffn_kernel, tiles_per_seq=seq_len // tile),
        grid=(n_tok // tile, n_chunks),
        in_specs=[
            tok_spec, vec_spec, vec_spec,
            up_cols(d_model, 0), up_cols(d_model, 1),
            up_cols(k_f, 0), up_cols(k_f, 1),
            up_cols(1, 0), up_cols(1, 1),
            pl.BlockSpec((None, chunk, d_model), lambda i, j: (layer, j, 0)),
        ],
        out_specs=tok_spec,
        out_shape=jax.ShapeDtypeStruct((n_tok, d_model), jnp.float32),
        scratch_shapes=[
            pltpu.VMEM((tile, d_model), jnp.bfloat16),
            pltpu.VMEM((n_groups, halo + GROUP, 2 * chunk), jnp.float32),
            pltpu.VMEM((tile, chunk), jnp.float32),
            pltpu.VMEM((tile, chunk), jnp.bfloat16),
            pltpu.VMEM((n_chunks, halo, 2 * chunk), jnp.float32),
        ],
        compiler_params=pltpu.CompilerParams(
            dimension_semantics=("arbitrary", "arbitrary"), vmem_limit_bytes=VMEM_LIMIT_BYTES),
        name="ffn",
    )(xs, p["norm_ffn_pre"], p["norm_ffn_post"], p["w_up"], p["w_up"], p["conv_ffn_w"],
      p["conv_ffn_w"], p["conv_ffn_bias"], p["conv_ffn_bias"], p["w_down"])


def _prepare_params(params):
    p = dict(params)
    for name in ("w_in", "w_out", "w_up", "w_down"):
        p[name] = params[name].astype(jnp.bfloat16)
    depth, n_pool, g_c, _ = params["pool_w"].shape
    pool_bd = jnp.zeros((depth, n_pool * g_c, n_pool * g_c), jnp.bfloat16)
    for g in range(n_pool):
        pool_bd = pool_bd.at[:, g * g_c:(g + 1) * g_c, g * g_c:(g + 1) * g_c].set(
            params["pool_w"][:, g].astype(jnp.bfloat16))
    p["pool_w"] = pool_bd
    for name in ("norm_mix_pre", "norm_mix_post", "norm_ffn_pre", "norm_ffn_post", "conv_b_bias",
                 "ln_b_gain", "ln_b_bias", "pool_scale", "conv_ffn_bias"):
        p[name] = params[name][:, None, :]
    return p


def _to_stream_order(x2d):
    n_tok, d_model = x2d.shape
    per_sub = GROUP // SUBLANES
    return x2d.reshape(n_tok // GROUP, SUBLANES, per_sub, d_model).swapaxes(1, 2).reshape(n_tok, d_model)


def _from_stream_order(xs):
    n_tok, d_model = xs.shape
    per_sub = GROUP // SUBLANES
    return xs.reshape(n_tok // GROUP, per_sub, SUBLANES, d_model).swapaxes(1, 2).reshape(n_tok, d_model)


def _trunk(x, params, ffn_groups, ffn_chunk):
    batch, seq_len, d_model = x.shape
    p = _prepare_params(params)
    xs = _to_stream_order(x.reshape(batch * seq_len, d_model))
    for layer in range(params["w_in"].shape[0]):
        xs = _mixer_layer(xs, layer, p, seq_len)
        xs = _ffn_layer(xs, layer, p, seq_len, ffn_groups, ffn_chunk)
    return _from_stream_order(xs).reshape(batch, seq_len, d_model)


def kernel(x, norm_mix_pre, norm_mix_post, norm_ffn_pre, norm_ffn_post, w_in, conv_a_w, conv_b_w,
           conv_b_bias, ln_b_gain, ln_b_bias, pool_w, pool_scale, w_out, w_up, conv_ffn_w,
           conv_ffn_bias, w_down):
    params = dict(
        norm_mix_pre=norm_mix_pre, norm_mix_post=norm_mix_post, norm_ffn_pre=norm_ffn_pre,
        norm_ffn_post=norm_ffn_post, w_in=w_in, conv_a_w=conv_a_w, conv_b_w=conv_b_w,
        conv_b_bias=conv_b_bias, ln_b_gain=ln_b_gain, ln_b_bias=ln_b_bias, pool_w=pool_w,
        pool_scale=pool_scale, w_out=w_out, w_up=w_up, conv_ffn_w=conv_ffn_w,
        conv_ffn_bias=conv_ffn_bias, w_down=w_down)
    return _trunk(x, params, FFN_GROUPS, FFN_CHUNK)
```

```python
import functools

import jax
import jax.numpy as jnp
from jax import lax
from jax.experimental import pallas as pl
from jax.experimental.pallas import tpu as pltpu

HEAD_DIM = 128
N_HEADS_A = 6
N_HEADS_B = 6
POOL_WINDOWS = (2, 4, 8, 16)
RMS_EPS = 1e-6
LN_EPS = 1e-5

SUBLANES = 8
LANES = 128
V7X_VMEM_BYTES = 64 * 1024 * 1024
VMEM_LIMIT_BYTES = V7X_VMEM_BYTES - 6 * 1024 * 1024

GROUP = 256
FFN_GROUPS = 4
FFN_CHUNK = 512
NORM_ROWS = 16
EW_ROWS = 32
CAST_COLS = 2048
ALL = slice(None)


def _blocks(n, size):
    return [slice(s, min(s + size, n)) for s in range(0, n, size)]


def _rms_norm(x, g8):
    n, c = x.shape
    ms = jnp.mean(x * x, axis=-1, keepdims=True)
    y = x * lax.rsqrt(ms + RMS_EPS)
    return (y.reshape(n // SUBLANES, SUBLANES, c) * g8[None]).reshape(n, c)


def _spread_gains(gain_scr, *gain_refs):
    for k, g_ref in enumerate(gain_refs):
        gain_scr[k] = jnp.broadcast_to(g_ref[...], gain_scr.shape[1:])


def _sigmoid(x):
    return 1.0 / (1.0 + jnp.exp(-x))


def _halo(e_ref):
    return e_ref.shape[1] - GROUP


def _tail_index(prefix, rows, lanes, offset=0):
    return prefix + (slice(offset + rows.start, offset + rows.stop), lanes)


def _wrap_history(e_ref, q, prev_ref, prev_prefix=(), prev_offset=0, lanes=ALL):
    for rows in _blocks(_halo(e_ref), EW_ROWS):
        n = rows.stop - rows.start
        new = e_ref[_tail_index((q,), rows, lanes, GROUP)]
        prev = prev_ref[_tail_index(prev_prefix, rows, lanes, prev_offset)]
        shape3 = (n // SUBLANES, SUBLANES, new.shape[-1])
        sub = lax.broadcasted_iota(jnp.int32, shape3, 1)
        mixed = jnp.where(sub == SUBLANES - 1, prev.reshape(shape3), new.reshape(shape3))
        e_ref[q, rows, lanes] = pltpu.roll(mixed, 1, axis=1).reshape(n, new.shape[-1])


def _copy_tail(e_ref, q, dst_ref, dst_prefix=(), lanes=ALL):
    for rows in _blocks(_halo(e_ref), EW_ROWS):
        dst_ref[_tail_index(dst_prefix, rows, lanes)] = e_ref[_tail_index((q,), rows, lanes, GROUP)]


def _taps(e_ref, q, w_ref, rows, lanes, w_lanes=None):
    w_lanes = lanes if w_lanes is None else w_lanes
    n_taps = w_ref.shape[0]
    acc = None
    for k in range(n_taps):
        start = _halo(e_ref) + rows.start - SUBLANES * (n_taps - 1 - k)
        term = w_ref[k:k + 1, w_lanes] * e_ref[q, start:start + rows.stop - rows.start, lanes]
        acc = term if acc is None else acc + term
    return acc


def _conv_columns(e_ref, q, w_ref, out_ref, lanes, delay_groups):
    halo = _halo(e_ref)
    n_taps = w_ref.shape[0]
    n_vregs = GROUP // SUBLANES
    for group_index, delays in enumerate(delay_groups):
        w = {d: w_ref[n_taps - 1 - d:n_taps - d, lanes] for d in delays}
        acc = {}
        for u in range(-max(delays), n_vregs):
            targets = [d for d in delays if 0 <= u + d < n_vregs]
            if not targets:
                continue
            z = e_ref[q, halo + SUBLANES * u:halo + SUBLANES * (u + 1), lanes]
            for d in targets:
                term = w[d] * z
                acc[u + d] = term if u + d not in acc else acc[u + d] + term
            done = u + min(delays)
            if 0 <= done < n_vregs:
                rows = slice(SUBLANES * done, SUBLANES * (done + 1))
                total = acc.pop(done)
                out_ref[rows, lanes] = total if group_index == 0 else total + out_ref[rows, lanes]


def _dot_into(dst_ref, lhs_ref, lhs_cols, rhs_ref, rhs_rows, rhs_cols, accumulate=False):
    out = jnp.dot(lhs_ref[:, lhs_cols], rhs_ref[rhs_rows, rhs_cols], preferred_element_type=jnp.float32)
    dst_ref[...] = out + dst_ref[...] if accumulate else out


def _cast_rows(src_ref, dst_ref):
    n_rows, n_cols = src_ref.shape
    for rows in _blocks(n_rows, 2 * SUBLANES):
        for cols in _blocks(n_cols, CAST_COLS):
            dst_ref[rows, cols] = src_ref[rows, cols].astype(jnp.bfloat16)


def _mixer_kernel(x_ref, xprev_ref, gpre_ref, gpost_ref, win_ref, caw_ref, cbw_ref, cbb_ref, lng_ref,
                  lnb_ref, pw_ref, ps_ref, wout_ref, wup_f32_ref, wdown_f32_ref, o_ref, wup_ref,
                  wdown_ref, gain_scr, h_scr, ua_scr, ub_scr, uc_scr, a_e, b_e, c_e, a_tail, b_tail,
                  c_tail, conv_scr, pool_scr, ycat, y_scr, *, groups_per_seq):
    wa = N_HEADS_A * HEAD_DIM
    wb = N_HEADS_B * HEAD_DIM
    wc = len(POOL_WINDOWS) * HEAD_DIM
    cols_a, cols_b, cols_c = slice(0, wa), slice(wa, wa + wb), slice(wa + wb, wa + wb + wc)
    off_b = 3 * wa
    off_c = off_b + 2 * wb
    step = pl.program_id(0)
    seq_group = step % groups_per_seq
    halo_a, halo_b, halo_c = _halo(a_e), _halo(b_e), _halo(c_e)

    @pl.when(step == 0)
    def _():
        for ref in (ycat, conv_scr, ua_scr, a_tail):
            ref[...] = jnp.zeros(ref.shape, ref.dtype)
        _spread_gains(gain_scr, gpre_ref, gpost_ref)

    @pl.when((step + groups_per_seq - 1) % groups_per_seq == 0)
    def _():
        a_tail[...] = jnp.zeros(a_tail.shape, jnp.float32)

    @pl.when(seq_group == 0)
    def _():
        for tail in (b_tail, c_tail):
            tail[...] = jnp.zeros(tail.shape, jnp.float32)

    norm_blocks = _blocks(GROUP, NORM_ROWS)
    ew_blocks = _blocks(GROUP, EW_ROWS)

    _cast_rows(wup_f32_ref, wup_ref)
    _cast_rows(wdown_f32_ref, wdown_ref)

    _dot_into(y_scr, ycat, cols_c, wout_ref, cols_c, ALL)

    for rows in ew_blocks:
        a_e[0, halo_a + rows.start:halo_a + rows.stop, :] = (
            ua_scr[rows, wa:2 * wa] * ua_scr[rows, 2 * wa:3 * wa])
    _wrap_history(a_e, 0, a_tail)
    _copy_tail(a_e, 0, a_tail)
    for rows in ew_blocks:
        ycat[rows, cols_a] = (ua_scr[rows, 0:wa] * _taps(a_e, 0, caw_ref, rows, ALL)).astype(jnp.bfloat16)
    _dot_into(y_scr, ycat, cols_a, wout_ref, cols_a, ALL, accumulate=True)

    for rows in ew_blocks:
        conv = conv_scr[rows, :] + cbb_ref[...]
        mu = jnp.mean(conv, axis=-1, keepdims=True)
        cen = conv - mu
        var = jnp.mean(cen * cen, axis=-1, keepdims=True)
        ln = cen * lax.rsqrt(var + LN_EPS) * lng_ref[...] + lnb_ref[...]
        ycat[rows, cols_b] = (ln * _sigmoid(ln)).astype(jnp.bfloat16)
    _dot_into(y_scr, ycat, cols_b, wout_ref, cols_b, ALL, accumulate=True)

    for rows in norm_blocks:
        h_scr[rows, :] = _rms_norm(x_ref[rows, :], gain_scr[0]).astype(jnp.bfloat16)
    _dot_into(ub_scr, h_scr, ALL, win_ref, ALL, slice(off_b, off_b + 2 * wb))

    for rows in norm_blocks:
        o_ref[rows, :] = xprev_ref[rows, :] + _rms_norm(y_scr[rows, :], gain_scr[1])

    for rows in ew_blocks:
        b_e[0, halo_b + rows.start:halo_b + rows.stop, :] = (
            ub_scr[rows, 0:wb] * _sigmoid(ub_scr[rows, wb:2 * wb]))
    _wrap_history(b_e, 0, b_tail)
    _copy_tail(b_e, 0, b_tail)
    _dot_into(uc_scr, h_scr, ALL, win_ref, ALL, slice(off_c, off_c + wc))

    def c_pool():
        for rows in _blocks(GROUP, 2 * EW_ROWS):
            c_e[0, halo_c + rows.start:halo_c + rows.stop, :] = uc_scr[rows, :]
        _wrap_history(c_e, 0, c_tail)
        _copy_tail(c_e, 0, c_tail)
        for g, window in enumerate(POOL_WINDOWS):
            lanes = slice(g * HEAD_DIM, (g + 1) * HEAD_DIM)
            for rows in _blocks(GROUP, 2 * EW_ROWS):
                n = rows.stop - rows.start
                base = halo_c + rows.start
                self_rows = c_e[0, base:base + n, lanes]
                total = self_rows
                for back in range(1, window):
                    total = total + c_e[0, base - SUBLANES * back:base - SUBLANES * back + n, lanes]
                row = rows.start + lax.broadcasted_iota(jnp.int32, (n, HEAD_DIM), 0)
                pos = seq_group * GROUP + (row % SUBLANES) * (GROUP // SUBLANES) + row // SUBLANES
                counts = jnp.minimum(pos + 1, window).astype(jnp.float32)
                pool_scr[rows, lanes] = (total / counts - self_rows).astype(jnp.bfloat16)

    @pl.when(step >= 0)
    def _():
        c_pool()
        k_b = cbw_ref.shape[0]
        halves = (range(0, k_b // 2 + 1), range(k_b // 2 + 1, k_b))
        heads_per_part = N_HEADS_B // 3
        for part in range(3):
            for head in range(part * heads_per_part, (part + 1) * heads_per_part):
                _conv_columns(b_e, 0, cbw_ref, conv_scr, slice(head * HEAD_DIM, (head + 1) * HEAD_DIM),
                              halves)
            cols = slice(part * wa, (part + 1) * wa)
            ua_scr[:, cols] = jnp.dot(h_scr[...], win_ref[:, cols], preferred_element_type=jnp.float32)
        mixed = jnp.dot(pool_scr[...], pw_ref[...], preferred_element_type=jnp.float32)
        ycat[:, cols_c] = (mixed * ps_ref[...]).astype(jnp.bfloat16)


def _mixer_layer(xs, layer, p, w_in, w_out, seq_len):
    n_tok, d_model = xs.shape
    assert seq_len % GROUP == 0 and n_tok % seq_len == 0
    wa = N_HEADS_A * HEAD_DIM
    wb = N_HEADS_B * HEAD_DIM
    wc = len(POOL_WINDOWS) * HEAD_DIM
    d_in = w_in.shape[-1]
    assert d_in == 3 * wa + 2 * wb + wc and w_out.shape[0] == wa + wb + wc
    d_ff = p["w_down"].shape[1]
    k_a = p["conv_a_w"].shape[1]
    k_b = p["conv_b_w"].shape[1]
    halo_a = SUBLANES * (k_a - 1)
    halo_b = SUBLANES * (k_b - 1)
    halo_c = SUBLANES * (max(POOL_WINDOWS) - 1)
    assert max(halo_a, halo_b, halo_c) <= GROUP
    n_groups = n_tok // GROUP
    up_rows = d_model // n_groups
    down_rows = 2 * d_ff // n_groups
    assert d_model % n_groups == 0 and (2 * d_ff) % n_groups == 0
    assert up_rows % (2 * SUBLANES) == 0 and down_rows % (2 * SUBLANES) == 0

    def layer_block(shape, **kw):
        zeros = (0,) * len(shape)
        return pl.BlockSpec((None,) + shape, lambda i: (layer,) + zeros, **kw)

    def up_slice(i):
        return jnp.minimum(i, n_groups - 1)

    def down_slice(i):
        return jnp.minimum(i, n_groups - 1) // 2

    resident = dict(pipeline_mode=pl.Buffered(1))
    f32, bf16 = jnp.float32, jnp.bfloat16
    return pl.pallas_call(
        functools.partial(_mixer_kernel, groups_per_seq=seq_len // GROUP),
        grid=(n_groups + 1,),
        in_specs=[
            pl.BlockSpec((GROUP, d_model), lambda i: (jnp.minimum(i, n_groups - 1), 0)),
            pl.BlockSpec((GROUP, d_model), lambda i: (jnp.maximum(i - 1, 0), 0)),
            layer_block((1, d_model)),
            layer_block((1, d_model)),
            pl.BlockSpec((d_model, d_in), lambda i: (0, 0), **resident),
            layer_block((k_a, wa)),
            layer_block((k_b, wb)),
            layer_block((1, wb)),
            layer_block((1, wb)),
            layer_block((1, wb)),
            layer_block((wc, wc)),
            layer_block((1, wc)),
            pl.BlockSpec((wa + wb + wc, d_model), lambda i: (0, 0), **resident),
            pl.BlockSpec((None, up_rows, 2 * d_ff), lambda i: (layer, up_slice(i), 0)),
            pl.BlockSpec((None, down_rows, d_model), lambda i: (layer, down_slice(i), 0)),
        ],
        out_specs=[
            pl.BlockSpec((GROUP, d_model), lambda i: (jnp.maximum(i - 1, 0), 0)),
            pl.BlockSpec((up_rows, 2 * d_ff), lambda i: (up_slice(i), 0)),
            pl.BlockSpec((down_rows, d_model), lambda i: (down_slice(i), 0)),
        ],
        out_shape=[
            jax.ShapeDtypeStruct((n_tok, d_model), f32),
            jax.ShapeDtypeStruct((d_model, 2 * d_ff), bf16),
            jax.ShapeDtypeStruct((d_ff, d_model), bf16),
        ],
        scratch_shapes=[
            pltpu.VMEM((2, SUBLANES, d_model), f32),
            pltpu.VMEM((GROUP, d_model), bf16),
            pltpu.VMEM((GROUP, 3 * wa), f32),
            pltpu.VMEM((GROUP, 2 * wb), f32),
            pltpu.VMEM((GROUP, wc), f32),
            pltpu.VMEM((1, halo_a + GROUP, wa), f32),
            pltpu.VMEM((1, halo_b + GROUP, wb), f32),
            pltpu.VMEM((1, halo_c + GROUP, wc), f32),
            pltpu.VMEM((halo_a, wa), f32),
            pltpu.VMEM((halo_b, wb), f32),
            pltpu.VMEM((halo_c, wc), f32),
            pltpu.VMEM((GROUP, wb), f32),
            pltpu.VMEM((GROUP, wc), bf16),
            pltpu.VMEM((GROUP, wa + wb + wc), bf16),
            pltpu.VMEM((GROUP, d_model), f32),
        ],
        compiler_params=pltpu.CompilerParams(
            dimension_semantics=("arbitrary",), vmem_limit_bytes=VMEM_LIMIT_BYTES),
        name="mixer",
    )(xs, xs, p["norm_mix_pre"], p["norm_mix_post"], w_in, p["conv_a_w"], p["conv_b_w"],
      p["conv_b_bias"], p["ln_b_gain"], p["ln_b_bias"], p["pool_w"], p["pool_scale"], w_out,
      p["w_up"], p["w_down"])


def _ffn_kernel(x_ref, gpre_ref, gpost_ref, wg_ref, wv_ref, cwg_ref, cwv_ref, cbg_ref, cbv_ref, wd_ref,
                *rest, tiles_per_seq, cast_next):
    if cast_next:
        win_f32_ref, wout_f32_ref, o_ref, win_ref, wout_ref = rest[:5]
        _cast_rows(win_f32_ref, win_ref)
        _cast_rows(wout_f32_ref, wout_ref)
        rest = (o_ref,) + rest[5:]
    o_ref, gain_scr, h_scr, up_e, silu_scr, act_scr, coef_scr, carry = rest
    n_groups = up_e.shape[0]
    halo = _halo(up_e)
    chunk = wd_ref.shape[0]
    tile = x_ref.shape[0]
    i = pl.program_id(0)
    j = pl.program_id(1)

    @pl.when(jnp.logical_and(i == 0, j == 0))
    def _():
        _spread_gains(gain_scr, gpre_ref, gpost_ref)

    @pl.when(j == 0)
    def _():
        for rows in _blocks(tile, NORM_ROWS):
            h_scr[rows, :] = _rms_norm(x_ref[rows, :], gain_scr[0]).astype(jnp.bfloat16)
            o_ref[rows, :] = jnp.zeros((rows.stop - rows.start, o_ref.shape[1]), jnp.float32)

    @pl.when(i % tiles_per_seq == 0)
    def _():
        carry[j] = jnp.zeros(carry.shape[1:], jnp.float32)

    def group_rows(q):
        return slice(q * GROUP, (q + 1) * GROUP)

    def project_up(w_ref, half):
        up = jnp.dot(h_scr[...], w_ref[...], preferred_element_type=jnp.float32)
        for q in range(n_groups):
            up_e[q, halo:halo + GROUP, half * chunk:(half + 1) * chunk] = up[group_rows(q), :]

    def wrap_group(q, lanes):
        if q == 0:
            _wrap_history(up_e, q, carry, (j,), lanes=lanes)
        else:
            _wrap_history(up_e, q, up_e, (q - 1,), GROUP, lanes=lanes)

    k_f = cwg_ref.shape[0]
    for half, (cw_ref, cb_ref) in enumerate(((cwg_ref, cbg_ref), (cwv_ref, cbv_ref))):
        for k in range(k_f):
            coef_scr[half, k] = jnp.broadcast_to(cw_ref[k:k + 1, :], (SUBLANES, chunk))
        coef_scr[half, k_f] = jnp.broadcast_to(cb_ref[...], (SUBLANES, chunk))

    def conv_bias(q, half, rows, lanes):
        n = rows.stop - rows.start
        shape3 = (n // SUBLANES, SUBLANES, lanes.stop - lanes.start)
        e_lanes = slice(half * chunk + lanes.start, half * chunk + lanes.stop)
        acc = coef_scr[half, k_f, :, lanes][None]
        for k in range(k_f):
            start = halo + rows.start - SUBLANES * (k_f - 1 - k)
            acc = acc + coef_scr[half, k, :, lanes][None] * up_e[q, start:start + n, e_lanes].reshape(shape3)
        return acc.reshape(n, shape3[2])

    def gate_part(q):
        wrap_group(q, slice(0, chunk))
        for rows in _blocks(GROUP, EW_ROWS):
            for lanes in _blocks(chunk, 2 * LANES):
                gate = conv_bias(q, 0, rows, lanes)
                silu_scr[q * GROUP + rows.start:q * GROUP + rows.stop, lanes] = gate * _sigmoid(gate)

    def value_part(q):
        wrap_group(q, slice(chunk, 2 * chunk))
        for rows in _blocks(GROUP, EW_ROWS):
            t_rows = slice(q * GROUP + rows.start, q * GROUP + rows.stop)
            for lanes in _blocks(chunk, 2 * LANES):
                val = conv_bias(q, 1, rows, lanes)
                act_scr[t_rows, lanes] = (silu_scr[t_rows, lanes] * val).astype(jnp.bfloat16)

    def project_down(rows):
        o_ref[rows, :] = (jnp.dot(act_scr[rows, :], wd_ref[...], preferred_element_type=jnp.float32)
                          + o_ref[rows, :])

    project_up(wg_ref, 0)
    for q in range(n_groups):
        gate_part(q)
    project_up(wv_ref, 1)
    halves = [range(0, (n_groups + 1) // 2), range((n_groups + 1) // 2, n_groups)]
    for half in halves:
        if len(half):
            for q in half:
                value_part(q)
            project_down(slice(half[0] * GROUP, (half[-1] + 1) * GROUP))
    _copy_tail(up_e, n_groups - 1, carry, (j,))

    @pl.when(j == pl.num_programs(1) - 1)
    def _():
        for rows in _blocks(tile, NORM_ROWS):
            o_ref[rows, :] = x_ref[rows, :] + _rms_norm(o_ref[rows, :], gain_scr[1])


def _ffn_layer(xs, layer, p, w_up, w_down, seq_len, n_groups, chunk, cast_next):
    n_tok, d_model = xs.shape
    tile = n_groups * GROUP
    d_ff = w_down.shape[0]
    assert seq_len % tile == 0 and n_tok % seq_len == 0 and d_ff % chunk == 0
    assert w_up.shape[-1] == 2 * d_ff
    n_chunks = d_ff // chunk
    n_tiles = n_tok // tile
    k_f = p["conv_ffn_w"].shape[1]
    halo = SUBLANES * (k_f - 1)
    assert halo <= GROUP

    tok_spec = pl.BlockSpec((tile, d_model), lambda i, j: (i, 0))
    vec_spec = pl.BlockSpec((None, 1, d_model), lambda i, j: (layer, 0, 0))

    def up_cols(rows, half):
        return pl.BlockSpec((None, rows, chunk), lambda i, j: (layer, 0, half * n_chunks + j))

    def w_up_cols(half):
        return pl.BlockSpec((d_model, chunk), lambda i, j: (0, half * n_chunks + j))

    in_specs = [
        tok_spec, vec_spec, vec_spec,
        w_up_cols(0), w_up_cols(1),
        up_cols(k_f, 0), up_cols(k_f, 1),
        up_cols(1, 0), up_cols(1, 1),
        pl.BlockSpec((chunk, d_model), lambda i, j: (j, 0)),
    ]
    operands = [xs, p["norm_ffn_pre"], p["norm_ffn_post"], w_up, w_up, p["conv_ffn_w"],
                p["conv_ffn_w"], p["conv_ffn_bias"], p["conv_ffn_bias"], w_down]
    out_specs = [tok_spec]
    out_shape = [jax.ShapeDtypeStruct((n_tok, d_model), jnp.float32)]
    if cast_next:
        cast_rows = 2 * SUBLANES
        while cast_rows * n_tiles * n_chunks < d_model:
            cast_rows *= 2
        n_slices = d_model // cast_rows
        assert d_model % cast_rows == 0
        d_in = p["w_in"].shape[-1]

        def row_slice(i, j):
            return jnp.minimum(i * n_chunks + j, n_slices - 1)

        for width, name in ((d_in, "w_in"), (d_model, "w_out")):
            assert p[name].shape[1] == d_model
            in_specs.append(pl.BlockSpec((None, cast_rows, width),
                                         lambda i, j: (layer + 1, row_slice(i, j), 0)))
            operands.append(p[name])
            out_specs.append(pl.BlockSpec((cast_rows, width), lambda i, j: (row_slice(i, j), 0)))
            out_shape.append(jax.ShapeDtypeStruct((d_model, width), jnp.bfloat16))

    return pl.pallas_call(
        functools.partial(_ffn_kernel, tiles_per_seq=seq_len // tile, cast_next=cast_next),
        grid=(n_tiles, n_chunks),
        in_specs=in_specs,
        out_specs=out_specs,
        out_shape=out_shape,
        scratch_shapes=[
            pltpu.VMEM((2, SUBLANES, d_model), jnp.float32),
            pltpu.VMEM((tile, d_model), jnp.bfloat16),
            pltpu.VMEM((n_groups, halo + GROUP, 2 * chunk), jnp.float32),
            pltpu.VMEM((tile, chunk), jnp.float32),
            pltpu.VMEM((tile, chunk), jnp.bfloat16),
            pltpu.VMEM((2, k_f + 1, SUBLANES, chunk), jnp.float32),
            pltpu.VMEM((n_chunks, halo, 2 * chunk), jnp.float32),
        ],
        compiler_params=pltpu.CompilerParams(
            dimension_semantics=("arbitrary", "arbitrary"), vmem_limit_bytes=VMEM_LIMIT_BYTES),
        name="ffn",
    )(*operands)


def _prepare_params(params):
    p = dict(params)
    depth, n_pool, g_c, _ = params["pool_w"].shape
    pool_bd = jnp.zeros((depth, n_pool * g_c, n_pool * g_c), jnp.bfloat16)
    for g in range(n_pool):
        pool_bd = pool_bd.at[:, g * g_c:(g + 1) * g_c, g * g_c:(g + 1) * g_c].set(
            params["pool_w"][:, g].astype(jnp.bfloat16))
    p["pool_w"] = pool_bd
    for name in ("norm_mix_pre", "norm_mix_post", "norm_ffn_pre", "norm_ffn_post", "conv_b_bias",
                 "ln_b_gain", "ln_b_bias", "pool_scale", "conv_ffn_bias"):
        p[name] = params[name][:, None, :]
    return p


def _to_stream_order(x2d):
    n_tok, d_model = x2d.shape
    per_sub = GROUP // SUBLANES
    return x2d.reshape(n_tok // GROUP, SUBLANES, per_sub, d_model).swapaxes(1, 2).reshape(n_tok, d_model)


def _from_stream_order(xs):
    n_tok, d_model = xs.shape
    per_sub = GROUP // SUBLANES
    return xs.reshape(n_tok // GROUP, per_sub, SUBLANES, d_model).swapaxes(1, 2).reshape(n_tok, d_model)


def _trunk(x, params, ffn_groups, ffn_chunk):
    batch, seq_len, d_model = x.shape
    p = _prepare_params(params)
    depth = params["w_in"].shape[0]
    xs = _to_stream_order(x.reshape(batch * seq_len, d_model))
    w_in = params["w_in"][0].astype(jnp.bfloat16)
    w_out = params["w_out"][0].astype(jnp.bfloat16)
    for layer in range(depth):
        xs, w_up, w_down = _mixer_layer(xs, layer, p, w_in, w_out, seq_len)
        cast_next = layer + 1 < depth
        outs = _ffn_layer(xs, layer, p, w_up, w_down, seq_len, ffn_groups, ffn_chunk, cast_next)
        xs = outs[0]
        if cast_next:
            w_in, w_out = outs[1:]
    return _from_stream_order(xs).reshape(batch, seq_len, d_model)


def kernel(x, norm_mix_pre, norm_mix_post, norm_ffn_pre, norm_ffn_post, w_in, conv_a_w, conv_b_w,
           conv_b_bias, ln_b_gain, ln_b_bias, pool_w, pool_scale, w_out, w_up, conv_ffn_w,
           conv_ffn_bias, w_down):
    params = dict(
        norm_mix_pre=norm_mix_pre, norm_mix_post=norm_mix_post, norm_ffn_pre=norm_ffn_pre,
        norm_ffn_post=norm_ffn_post, w_in=w_in, conv_a_w=conv_a_w, conv_b_w=conv_b_w,
        conv_b_bias=conv_b_bias, ln_b_gain=ln_b_gain, ln_b_bias=ln_b_bias, pool_w=pool_w,
        pool_scale=pool_scale, w_out=w_out, w_up=w_up, conv_ffn_w=conv_ffn_w,
        conv_ffn_bias=conv_ffn_bias, w_down=w_down)
    return _trunk(x, params, FFN_GROUPS, FFN_CHUNK)
```

```python
import functools

import jax
import jax.numpy as jnp
from jax import lax
from jax.experimental import pallas as pl
from jax.experimental.pallas import tpu as pltpu

HEAD_DIM = 128
N_HEADS_A = 6
N_HEADS_B = 6
POOL_WINDOWS = (2, 4, 8, 16)
RMS_EPS = 1e-6
LN_EPS = 1e-5

SUBLANES = 8
LANES = 128
V7X_VMEM_BYTES = 64 * 1024 * 1024
VMEM_LIMIT_BYTES = V7X_VMEM_BYTES - 6 * 1024 * 1024

GROUP = 256
FFN_GROUPS = 4
FFN_CHUNK = 512
NORM_ROWS = 16
EW_ROWS = 32
SMALL_ROWS = 16
CAST_COLS = 2048
ALL = slice(None)


def _blocks(n, size):
    return [slice(s, min(s + size, n)) for s in range(0, n, size)]


def _rms_norm(x, g8):
    n, c = x.shape
    ms = jnp.mean(x * x, axis=-1, keepdims=True)
    y = x * lax.rsqrt(ms + RMS_EPS)
    return (y.reshape(n // SUBLANES, SUBLANES, c) * g8[None]).reshape(n, c)


def _spread_gains(gain_scr, *gain_refs):
    for k, g_ref in enumerate(gain_refs):
        gain_scr[k] = jnp.broadcast_to(g_ref[...], gain_scr.shape[1:])


def _sigmoid(x):
    return 0.5 * jnp.tanh(0.5 * x) + 0.5


def _halo(e_ref):
    return e_ref.shape[1] - GROUP


def _tail_index(prefix, rows, lanes, offset=0):
    return prefix + (slice(offset + rows.start, offset + rows.stop), lanes)


def _wrap_history(e_ref, q, prev_ref, prev_prefix=(), prev_offset=0, lanes=ALL):
    for rows in _blocks(_halo(e_ref), EW_ROWS):
        n = rows.stop - rows.start
        new = e_ref[_tail_index((q,), rows, lanes, GROUP)]
        prev = prev_ref[_tail_index(prev_prefix, rows, lanes, prev_offset)]
        shape3 = (n // SUBLANES, SUBLANES, new.shape[-1])
        sub = lax.broadcasted_iota(jnp.int32, shape3, 1)
        mixed = jnp.where(sub == SUBLANES - 1, prev.reshape(shape3), new.reshape(shape3))
        e_ref[q, rows, lanes] = pltpu.roll(mixed, 1, axis=1).reshape(n, new.shape[-1])


def _copy_tail(e_ref, q, dst_ref, dst_prefix=(), lanes=ALL):
    for rows in _blocks(_halo(e_ref), EW_ROWS):
        dst_ref[_tail_index(dst_prefix, rows, lanes)] = e_ref[_tail_index((q,), rows, lanes, GROUP)]


def _taps(e_ref, q, w_ref, rows, lanes, w_lanes=None):
    w_lanes = lanes if w_lanes is None else w_lanes
    n_taps = w_ref.shape[0]
    acc = None
    for k in range(n_taps):
        start = _halo(e_ref) + rows.start - SUBLANES * (n_taps - 1 - k)
        term = w_ref[k:k + 1, w_lanes] * e_ref[q, start:start + rows.stop - rows.start, lanes]
        acc = term if acc is None else acc + term
    return acc


def _conv_columns(e_ref, q, w_ref, out_ref, lanes, delay_groups):
    halo = _halo(e_ref)
    n_taps = w_ref.shape[0]
    n_vregs = GROUP // SUBLANES
    for group_index, delays in enumerate(delay_groups):
        w = {d: w_ref[n_taps - 1 - d:n_taps - d, lanes] for d in delays}
        acc = {}
        for u in range(-max(delays), n_vregs):
            targets = [d for d in delays if 0 <= u + d < n_vregs]
            if not targets:
                continue
            z = e_ref[q, halo + SUBLANES * u:halo + SUBLANES * (u + 1), lanes]
            for d in targets:
                term = w[d] * z
                acc[u + d] = term if u + d not in acc else acc[u + d] + term
            done = u + min(delays)
            if 0 <= done < n_vregs:
                rows = slice(SUBLANES * done, SUBLANES * (done + 1))
                total = acc.pop(done)
                out_ref[rows, lanes] = total if group_index == 0 else total + out_ref[rows, lanes]


def _dot_into(dst_ref, lhs_ref, lhs_cols, rhs_ref, rhs_rows, rhs_cols, accumulate=False):
    out = jnp.dot(lhs_ref[:, lhs_cols], rhs_ref[rhs_rows, rhs_cols], preferred_element_type=jnp.float32)
    dst_ref[...] = out + dst_ref[...] if accumulate else out


def _cast_rows(src_ref, dst_ref):
    n_rows, n_cols = src_ref.shape
    for rows in _blocks(n_rows, 2 * SUBLANES):
        for cols in _blocks(n_cols, CAST_COLS):
            dst_ref[rows, cols] = src_ref[rows, cols].astype(jnp.bfloat16)


def _mixer_kernel(x_ref, xprev_ref, gpre_ref, gpost_ref, win_ref, caw_ref, cbw_ref, cbb_ref, lng_ref,
                  lnb_ref, pw_ref, ps_ref, wout_ref, wup_f32_ref, wdown_f32_ref, o_ref, wup_ref,
                  wdown_ref, gain_scr, h_scr, ua_scr, ub_scr, uc_scr, a_e, b_e, c_e, a_tail, b_tail,
                  c_tail, conv_scr, pool_scr, ycat, y_scr, *, groups_per_seq):
    wa = N_HEADS_A * HEAD_DIM
    wb = N_HEADS_B * HEAD_DIM
    wc = len(POOL_WINDOWS) * HEAD_DIM
    cols_a, cols_b, cols_c = slice(0, wa), slice(wa, wa + wb), slice(wa + wb, wa + wb + wc)
    off_b = 3 * wa
    off_c = off_b + 2 * wb
    step = pl.program_id(0)
    seq_group = step % groups_per_seq
    halo_a, halo_b, halo_c = _halo(a_e), _halo(b_e), _halo(c_e)

    @pl.when(step == 0)
    def _():
        for ref in (ycat, conv_scr, ua_scr, a_tail):
            ref[...] = jnp.zeros(ref.shape, ref.dtype)
        _spread_gains(gain_scr, gpre_ref, gpost_ref)

    @pl.when((step + groups_per_seq - 1) % groups_per_seq == 0)
    def _():
        a_tail[...] = jnp.zeros(a_tail.shape, jnp.float32)

    @pl.when(seq_group == 0)
    def _():
        for tail in (b_tail, c_tail):
            tail[...] = jnp.zeros(tail.shape, jnp.float32)

    norm_blocks = _blocks(GROUP, NORM_ROWS)
    ew_blocks = _blocks(GROUP, EW_ROWS)

    _cast_rows(wup_f32_ref, wup_ref)
    _cast_rows(wdown_f32_ref, wdown_ref)

    _dot_into(y_scr, ycat, cols_c, wout_ref, cols_c, ALL)

    for rows in ew_blocks:
        a_e[0, halo_a + rows.start:halo_a + rows.stop, :] = (
            ua_scr[rows, wa:2 * wa] * ua_scr[rows, 2 * wa:3 * wa])
    _wrap_history(a_e, 0, a_tail)
    _copy_tail(a_e, 0, a_tail)
    for rows in ew_blocks:
        ycat[rows, cols_a] = (ua_scr[rows, 0:wa] * _taps(a_e, 0, caw_ref, rows, ALL)).astype(jnp.bfloat16)
    _dot_into(y_scr, ycat, cols_a, wout_ref, cols_a, ALL, accumulate=True)

    for rows in ew_blocks:
        conv = conv_scr[rows, :] + cbb_ref[...]
        mu = jnp.mean(conv, axis=-1, keepdims=True)
        cen = conv - mu
        var = jnp.mean(cen * cen, axis=-1, keepdims=True)
        ln = cen * lax.rsqrt(var + LN_EPS) * lng_ref[...] + lnb_ref[...]
        ycat[rows, cols_b] = (ln * _sigmoid(ln)).astype(jnp.bfloat16)
    _dot_into(y_scr, ycat, cols_b, wout_ref, cols_b, ALL, accumulate=True)

    for rows in norm_blocks:
        h_scr[rows, :] = _rms_norm(x_ref[rows, :], gain_scr[0]).astype(jnp.bfloat16)
    _dot_into(ub_scr, h_scr, ALL, win_ref, ALL, slice(off_b, off_b + 2 * wb))

    for rows in norm_blocks:
        o_ref[rows, :] = xprev_ref[rows, :] + _rms_norm(y_scr[rows, :], gain_scr[1])

    for rows in ew_blocks:
        b_e[0, halo_b + rows.start:halo_b + rows.stop, :] = (
            ub_scr[rows, 0:wb] * _sigmoid(ub_scr[rows, wb:2 * wb]))
    _wrap_history(b_e, 0, b_tail)
    _copy_tail(b_e, 0, b_tail)
    _dot_into(uc_scr, h_scr, ALL, win_ref, ALL, slice(off_c, off_c + wc))

    def c_pool():
        for rows in _blocks(GROUP, SMALL_ROWS):
            c_e[0, halo_c + rows.start:halo_c + rows.stop, :] = uc_scr[rows, :]
        _wrap_history(c_e, 0, c_tail)
        _copy_tail(c_e, 0, c_tail)
        for g, window in enumerate(POOL_WINDOWS):
            lanes = slice(g * HEAD_DIM, (g + 1) * HEAD_DIM)
            for rows in _blocks(GROUP, SMALL_ROWS):
                n = rows.stop - rows.start
                base = halo_c + rows.start
                self_rows = c_e[0, base:base + n, lanes]
                total = self_rows
                for back in range(1, window):
                    total = total + c_e[0, base - SUBLANES * back:base - SUBLANES * back + n, lanes]
                row = rows.start + lax.broadcasted_iota(jnp.int32, (n, HEAD_DIM), 0)
                pos = seq_group * GROUP + (row % SUBLANES) * (GROUP // SUBLANES) + row // SUBLANES
                counts = jnp.minimum(pos + 1, window).astype(jnp.float32)
                pool_scr[rows, lanes] = (total / counts - self_rows).astype(jnp.bfloat16)

    @pl.when(step >= 0)
    def _():
        c_pool()
        k_b = cbw_ref.shape[0]
        halves = (range(0, k_b // 2 + 1), range(k_b // 2 + 1, k_b))
        heads_per_part = N_HEADS_B // 3
        for part in range(3):
            for head in range(part * heads_per_part, (part + 1) * heads_per_part):
                _conv_columns(b_e, 0, cbw_ref, conv_scr, slice(head * HEAD_DIM, (head + 1) * HEAD_DIM),
                              halves)
            cols = slice(part * wa, (part + 1) * wa)
            ua_scr[:, cols] = jnp.dot(h_scr[...], win_ref[:, cols], preferred_element_type=jnp.float32)
        mixed = jnp.dot(pool_scr[...], pw_ref[...], preferred_element_type=jnp.float32)
        ycat[:, cols_c] = (mixed * ps_ref[...]).astype(jnp.bfloat16)


def _mixer_layer(xs, layer, p, w_in, w_out, seq_len):
    n_tok, d_model = xs.shape
    assert seq_len % GROUP == 0 and n_tok % seq_len == 0
    wa = N_HEADS_A * HEAD_DIM
    wb = N_HEADS_B * HEAD_DIM
    wc = len(POOL_WINDOWS) * HEAD_DIM
    d_in = w_in.shape[-1]
    assert d_in == 3 * wa + 2 * wb + wc and w_out.shape[0] == wa + wb + wc
    d_ff = p["w_down"].shape[1]
    k_a = p["conv_a_w"].shape[1]
    k_b = p["conv_b_w"].shape[1]
    halo_a = SUBLANES * (k_a - 1)
    halo_b = SUBLANES * (k_b - 1)
    halo_c = SUBLANES * (max(POOL_WINDOWS) - 1)
    assert max(halo_a, halo_b, halo_c) <= GROUP
    n_groups = n_tok // GROUP
    up_rows = d_model // n_groups
    down_rows = 2 * d_ff // n_groups
    assert d_model % n_groups == 0 and (2 * d_ff) % n_groups == 0
    assert up_rows % (2 * SUBLANES) == 0 and down_rows % (2 * SUBLANES) == 0

    def layer_block(shape, **kw):
        zeros = (0,) * len(shape)
        return pl.BlockSpec((None,) + shape, lambda i: (layer,) + zeros, **kw)

    def up_slice(i):
        return jnp.minimum(i, n_groups - 1)

    def down_slice(i):
        return jnp.minimum(i, n_groups - 1) // 2

    resident = dict(pipeline_mode=pl.Buffered(1))
    f32, bf16 = jnp.float32, jnp.bfloat16
    return pl.pallas_call(
        functools.partial(_mixer_kernel, groups_per_seq=seq_len // GROUP),
        grid=(n_groups + 1,),
        in_specs=[
            pl.BlockSpec((GROUP, d_model), lambda i: (jnp.minimum(i, n_groups - 1), 0)),
            pl.BlockSpec((GROUP, d_model), lambda i: (jnp.maximum(i - 1, 0), 0)),
            layer_block((1, d_model)),
            layer_block((1, d_model)),
            pl.BlockSpec((d_model, d_in), lambda i: (0, 0), **resident),
            layer_block((k_a, wa)),
            layer_block((k_b, wb)),
            layer_block((1, wb)),
            layer_block((1, wb)),
            layer_block((1, wb)),
            layer_block((wc, wc)),
            layer_block((1, wc)),
            pl.BlockSpec((wa + wb + wc, d_model), lambda i: (0, 0), **resident),
            pl.BlockSpec((None, up_rows, 2 * d_ff), lambda i: (layer, up_slice(i), 0)),
            pl.BlockSpec((None, down_rows, d_model), lambda i: (layer, down_slice(i), 0)),
        ],
        out_specs=[
            pl.BlockSpec((GROUP, d_model), lambda i: (jnp.maximum(i - 1, 0), 0)),
            pl.BlockSpec((up_rows, 2 * d_ff), lambda i: (up_slice(i), 0)),
            pl.BlockSpec((down_rows, d_model), lambda i: (down_slice(i), 0)),
        ],
        out_shape=[
            jax.ShapeDtypeStruct((n_tok, d_model), f32),
            jax.ShapeDtypeStruct((d_model, 2 * d_ff), bf16),
            jax.ShapeDtypeStruct((d_ff, d_model), bf16),
        ],
        scratch_shapes=[
            pltpu.VMEM((2, SUBLANES, d_model), f32),
            pltpu.VMEM((GROUP, d_model), bf16),
            pltpu.VMEM((GROUP, 3 * wa), f32),
            pltpu.VMEM((GROUP, 2 * wb), f32),
            pltpu.VMEM((GROUP, wc), f32),
            pltpu.VMEM((1, halo_a + GROUP, wa), f32),
            pltpu.VMEM((1, halo_b + GROUP, wb), f32),
            pltpu.VMEM((1, halo_c + GROUP, wc), f32),
            pltpu.VMEM((halo_a, wa), f32),
            pltpu.VMEM((halo_b, wb), f32),
            pltpu.VMEM((halo_c, wc), f32),
            pltpu.VMEM((GROUP, wb), f32),
            pltpu.VMEM((GROUP, wc), bf16),
            pltpu.VMEM((GROUP, wa + wb + wc), bf16),
            pltpu.VMEM((GROUP, d_model), f32),
        ],
        compiler_params=pltpu.CompilerParams(
            dimension_semantics=("arbitrary",), vmem_limit_bytes=VMEM_LIMIT_BYTES),
        name="mixer",
    )(xs, xs, p["norm_mix_pre"], p["norm_mix_post"], w_in, p["conv_a_w"], p["conv_b_w"],
      p["conv_b_bias"], p["ln_b_gain"], p["ln_b_bias"], p["pool_w"], p["pool_scale"], w_out,
      p["w_up"], p["w_down"])


def _ffn_kernel(x_ref, gpre_ref, gpost_ref, wg_ref, wv_ref, cwg_ref, cwv_ref, cbg_ref, cbv_ref, wd_ref,
                *rest, tiles_per_seq, cast_next):
    if cast_next:
        win_f32_ref, wout_f32_ref, o_ref, win_ref, wout_ref = rest[:5]
        _cast_rows(win_f32_ref, win_ref)
        _cast_rows(wout_f32_ref, wout_ref)
        rest = (o_ref,) + rest[5:]
    o_ref, gain_scr, h_scr, up_e, silu_scr, act_scr, coef_scr, carry = rest
    n_groups = up_e.shape[0]
    halo = _halo(up_e)
    chunk = wd_ref.shape[0]
    tile = x_ref.shape[0]
    i = pl.program_id(0)
    j = pl.program_id(1)

    @pl.when(jnp.logical_and(i == 0, j == 0))
    def _():
        _spread_gains(gain_scr, gpre_ref, gpost_ref)

    @pl.when(j == 0)
    def _():
        for rows in _blocks(tile, NORM_ROWS):
            h_scr[rows, :] = _rms_norm(x_ref[rows, :], gain_scr[0]).astype(jnp.bfloat16)
            o_ref[rows, :] = jnp.zeros((rows.stop - rows.start, o_ref.shape[1]), jnp.float32)

    @pl.when(i % tiles_per_seq == 0)
    def _():
        carry[j] = jnp.zeros(carry.shape[1:], jnp.float32)

    def group_rows(q):
        return slice(q * GROUP, (q + 1) * GROUP)

    def project_up(w_ref, half):
        up = jnp.dot(h_scr[...], w_ref[...], preferred_element_type=jnp.float32)
        for q in range(n_groups):
            up_e[q, halo:halo + GROUP, half * chunk:(half + 1) * chunk] = up[group_rows(q), :]

    def wrap_group(q, lanes):
        if q == 0:
            _wrap_history(up_e, q, carry, (j,), lanes=lanes)
        else:
            _wrap_history(up_e, q, up_e, (q - 1,), GROUP, lanes=lanes)

    k_f = cwg_ref.shape[0]
    for half, (cw_ref, cb_ref) in enumerate(((cwg_ref, cbg_ref), (cwv_ref, cbv_ref))):
        for k in range(k_f):
            coef_scr[half, k] = jnp.broadcast_to(cw_ref[k:k + 1, :], (SUBLANES, chunk))
        coef_scr[half, k_f] = jnp.broadcast_to(cb_ref[...], (SUBLANES, chunk))

    def conv_bias(q, half, rows, lanes):
        n = rows.stop - rows.start
        shape3 = (n // SUBLANES, SUBLANES, lanes.stop - lanes.start)
        e_lanes = slice(half * chunk + lanes.start, half * chunk + lanes.stop)
        acc = coef_scr[half, k_f, :, lanes][None]
        for k in range(k_f):
            start = halo + rows.start - SUBLANES * (k_f - 1 - k)
            acc = acc + coef_scr[half, k, :, lanes][None] * up_e[q, start:start + n, e_lanes].reshape(shape3)
        return acc.reshape(n, shape3[2])

    def gate_part(q):
        wrap_group(q, slice(0, chunk))
        for rows in _blocks(GROUP, SMALL_ROWS):
            for lanes in _blocks(chunk, 2 * LANES):
                gate = conv_bias(q, 0, rows, lanes)
                silu_scr[q * GROUP + rows.start:q * GROUP + rows.stop, lanes] = gate * _sigmoid(gate)

    def value_part(q):
        wrap_group(q, slice(chunk, 2 * chunk))
        for rows in _blocks(GROUP, SMALL_ROWS):
            t_rows = slice(q * GROUP + rows.start, q * GROUP + rows.stop)
            for lanes in _blocks(chunk, 2 * LANES):
                val = conv_bias(q, 1, rows, lanes)
                act_scr[t_rows, lanes] = (silu_scr[t_rows, lanes] * val).astype(jnp.bfloat16)

    def project_down(rows):
        o_ref[rows, :] = (jnp.dot(act_scr[rows, :], wd_ref[...], preferred_element_type=jnp.float32)
                          + o_ref[rows, :])

    project_up(wg_ref, 0)
    for q in range(n_groups):
        gate_part(q)
    project_up(wv_ref, 1)
    halves = [range(0, (n_groups + 1) // 2), range((n_groups + 1) // 2, n_groups)]
    for half in halves:
        if len(half):
            for q in half:
                value_part(q)
            project_down(slice(half[0] * GROUP, (half[-1] + 1) * GROUP))
    _copy_tail(up_e, n_groups - 1, carry, (j,))

    @pl.when(j == pl.num_programs(1) - 1)
    def _():
        for rows in _blocks(tile, NORM_ROWS):
            o_ref[rows, :] = x_ref[rows, :] + _rms_norm(o_ref[rows, :], gain_scr[1])


def _ffn_layer(xs, layer, p, w_up, w_down, seq_len, n_groups, chunk, cast_next):
    n_tok, d_model = xs.shape
    tile = n_groups * GROUP
    d_ff = w_down.shape[0]
    assert seq_len % tile == 0 and n_tok % seq_len == 0 and d_ff % chunk == 0
    assert w_up.shape[-1] == 2 * d_ff
    n_chunks = d_ff // chunk
    n_tiles = n_tok // tile
    k_f = p["conv_ffn_w"].shape[1]
    halo = SUBLANES * (k_f - 1)
    assert halo <= GROUP

    tok_spec = pl.BlockSpec((tile, d_model), lambda i, j: (i, 0))
    vec_spec = pl.BlockSpec((None, 1, d_model), lambda i, j: (layer, 0, 0))

    def up_cols(rows, half):
        return pl.BlockSpec((None, rows, chunk), lambda i, j: (layer, 0, half * n_chunks + j))

    def w_up_cols(half):
        return pl.BlockSpec((d_model, chunk), lambda i, j: (0, half * n_chunks + j))

    in_specs = [
        tok_spec, vec_spec, vec_spec,
        w_up_cols(0), w_up_cols(1),
        up_cols(k_f, 0), up_cols(k_f, 1),
        up_cols(1, 0), up_cols(1, 1),
        pl.BlockSpec((chunk, d_model), lambda i, j: (j, 0)),
    ]
    operands = [xs, p["norm_ffn_pre"], p["norm_ffn_post"], w_up, w_up, p["conv_ffn_w"],
                p["conv_ffn_w"], p["conv_ffn_bias"], p["conv_ffn_bias"], w_down]
    out_specs = [tok_spec]
    out_shape = [jax.ShapeDtypeStruct((n_tok, d_model), jnp.float32)]
    if cast_next:
        cast_rows = 2 * SUBLANES
        while cast_rows * n_tiles * n_chunks < d_model:
            cast_rows *= 2
        n_slices = d_model // cast_rows
        assert d_model % cast_rows == 0
        d_in = p["w_in"].shape[-1]

        def row_slice(i, j):
            return jnp.minimum(i * n_chunks + j, n_slices - 1)

        for width, name in ((d_in, "w_in"), (d_model, "w_out")):
            assert p[name].shape[1] == d_model
            in_specs.append(pl.BlockSpec((None, cast_rows, width),
                                         lambda i, j: (layer + 1, row_slice(i, j), 0)))
            operands.append(p[name])
            out_specs.append(pl.BlockSpec((cast_rows, width), lambda i, j: (row_slice(i, j), 0)))
            out_shape.append(jax.ShapeDtypeStruct((d_model, width), jnp.bfloat16))

    return pl.pallas_call(
        functools.partial(_ffn_kernel, tiles_per_seq=seq_len // tile, cast_next=cast_next),
        grid=(n_tiles, n_chunks),
        in_specs=in_specs,
        out_specs=out_specs,
        out_shape=out_shape,
        scratch_shapes=[
            pltpu.VMEM((2, SUBLANES, d_model), jnp.float32),
            pltpu.VMEM((tile, d_model), jnp.bfloat16),
            pltpu.VMEM((n_groups, halo + GROUP, 2 * chunk), jnp.float32),
            pltpu.VMEM((tile, chunk), jnp.float32),
            pltpu.VMEM((tile, chunk), jnp.bfloat16),
            pltpu.VMEM((2, k_f + 1, SUBLANES, chunk), jnp.float32),
            pltpu.VMEM((n_chunks, halo, 2 * chunk), jnp.float32),
        ],
        compiler_params=pltpu.CompilerParams(
            dimension_semantics=("arbitrary", "arbitrary"), vmem_limit_bytes=VMEM_LIMIT_BYTES),
        name="ffn",
    )(*operands)


def _prepare_params(params):
    p = dict(params)
    depth, n_pool, g_c, _ = params["pool_w"].shape
    pool_bd = jnp.zeros((depth, n_pool * g_c, n_pool * g_c), jnp.bfloat16)
    for g in range(n_pool):
        pool_bd = pool_bd.at[:, g * g_c:(g + 1) * g_c, g * g_c:(g + 1) * g_c].set(
            params["pool_w"][:, g].astype(jnp.bfloat16))
    p["pool_w"] = pool_bd
    for name in ("norm_mix_pre", "norm_mix_post", "norm_ffn_pre", "norm_ffn_post", "conv_b_bias",
                 "ln_b_gain", "ln_b_bias", "pool_scale", "conv_ffn_bias"):
        p[name] = params[name][:, None, :]
    return p


def _to_stream_order(x2d):
    n_tok, d_model = x2d.shape
    per_sub = GROUP // SUBLANES
    return x2d.reshape(n_tok // GROUP, SUBLANES, per_sub, d_model).swapaxes(1, 2).reshape(n_tok, d_model)


def _from_stream_order(xs):
    n_tok, d_model = xs.shape
    per_sub = GROUP // SUBLANES
    return xs.reshape(n_tok // GROUP, per_sub, SUBLANES, d_model).swapaxes(1, 2).reshape(n_tok, d_model)


def _trunk(x, params, ffn_groups, ffn_chunk):
    batch, seq_len, d_model = x.shape
    p = _prepare_params(params)
    depth = params["w_in"].shape[0]
    xs = _to_stream_order(x.reshape(batch * seq_len, d_model))
    w_in = params["w_in"][0].astype(jnp.bfloat16)
    w_out = params["w_out"][0].astype(jnp.bfloat16)
    for layer in range(depth):
        xs, w_up, w_down = _mixer_layer(xs, layer, p, w_in, w_out, seq_len)
        cast_next = layer + 1 < depth
        outs = _ffn_layer(xs, layer, p, w_up, w_down, seq_len, ffn_groups, ffn_chunk, cast_next)
        xs = outs[0]
        if cast_next:
            w_in, w_out = outs[1:]
    return _from_stream_order(xs).reshape(batch, seq_len, d_model)


def kernel(x, norm_mix_pre, norm_mix_post, norm_ffn_pre, norm_ffn_post, w_in, conv_a_w, conv_b_w,
           conv_b_bias, ln_b_gain, ln_b_bias, pool_w, pool_scale, w_out, w_up, conv_ffn_w,
           conv_ffn_bias, w_down):
    params = dict(
        norm_mix_pre=norm_mix_pre, norm_mix_post=norm_mix_post, norm_ffn_pre=norm_ffn_pre,
        norm_ffn_post=norm_ffn_post, w_in=w_in, conv_a_w=conv_a_w, conv_b_w=conv_b_w,
        conv_b_bias=conv_b_bias, ln_b_gain=ln_b_gain, ln_b_bias=ln_b_bias, pool_w=pool_w,
        pool_scale=pool_scale, w_out=w_out, w_up=w_up, conv_ffn_w=conv_ffn_w,
        conv_ffn_bias=conv_ffn_bias, w_down=w_down)
    return _trunk(x, params, FFN_GROUPS, FFN_CHUNK)
```
